```python
import math
import jax, jax.numpy as jnp
from jax import lax
import numpy as np

D_MODEL = 2048
BATCH = 4
SEQ = 2048
DEPTH = 4
DEC_BATCH = 128
DEC_SEQ = 1
PAST_LEN = 8192
PAGE_SIZE = 128

D_FF = 5632
WIDTH_A = D_MODEL // 2
GROUPS_A = 4
CHUNK = 128
WIDTH_B = D_MODEL // 2
POOL_WINDOWS = (2, 4, 8, 16)
GROUPS_B = len(POOL_WINDOWS)
POOL_BUF = max(POOL_WINDOWS) - 1
N_HEADS = 16
QK_NOPE = 128
QK_ROPE = 64
QK_DIM = QK_NOPE + QK_ROPE
V_DIM = 128
Q_LORA = 512
KV_LORA = 512
ROPE_THETA = 10000.0
Q_BLOCK = 128
ATTN_SCALE = QK_DIM ** -0.5
N_BRANCH = 3
EPS = 1e-6
OFF_B = 2 * WIDTH_A
OFF_Q = OFF_B + WIDTH_B
OFF_KV = OFF_Q + Q_LORA
OFF_KR = OFF_KV + KV_LORA
OFF_G = OFF_KR + QK_ROPE
N_IN = OFF_G + N_BRANCH * D_MODEL

kernel_name = 'hybrid_gmlp_pool_mla_macaron_step'

F32 = jnp.float32


def rms_norm(x, g):
    x32 = x.astype(F32)
    y = x32 * lax.rsqrt(jnp.mean(x32 * x32, axis=-1, keepdims=True) + EPS)
    return (y * g.astype(F32)).astype(x.dtype)


def swiglu(x, w_up, w_down):
    g, u = jnp.split(x @ w_up, 2, axis=-1)
    return (jax.nn.silu(g) * u) @ w_down


def rope(x, pos):
    half = QK_ROPE // 2
    inv = jnp.power(ROPE_THETA, -jnp.arange(half, dtype=F32) / half)
    ang = pos.astype(F32)[:, None] * inv[None, :]
    shape = (ang.shape[0],) + (1,) * (x.ndim - 3) + (half,)
    cos = jnp.cos(ang).reshape(shape)
    sin = jnp.sin(ang).reshape(shape)
    x32 = x.astype(F32)
    x1, x2 = x32[..., :half], x32[..., half:]
    return jnp.concatenate([x1 * cos - x2 * sin, x2 * cos + x1 * sin], axis=-1).astype(x.dtype)


def gmlp_mixer(z_a, v_gain, w_s, b_s):
    z = jax.nn.gelu(z_a)
    u, v = jnp.split(z, 2, axis=-1)
    v = rms_norm(v, v_gain)
    B, T, _ = u.shape
    L = min(T, CHUNK)
    mask = jnp.tril(jnp.ones((L, L), dtype=bool))
    ws = jnp.where(mask, w_s[:, :L, :L], 0).astype(v.dtype)
    vc = v.reshape(B, T // L, L, GROUPS_A, WIDTH_A // GROUPS_A)
    s = jnp.einsum('gts,bcsgd->bctgd', ws, vc) + b_s[:, :L].T[None, None, :, :, None].astype(v.dtype)
    return u * s.reshape(B, T, WIDTH_A), v


def pool_mixer(p, prev, pos0, w_grp, scale):
    B, T, _ = p.shape
    xp = jnp.concatenate([prev.astype(p.dtype), p], axis=1)
    cs = jnp.cumsum(xp.astype(F32), axis=1)
    cs = jnp.concatenate([jnp.zeros((B, 1, WIDTH_B), F32), cs], axis=1)
    t = jnp.arange(T)
    dg = WIDTH_B // GROUPS_B
    outs = []
    for gi, w in enumerate(POOL_WINDOWS):
        sl = slice(gi * dg, (gi + 1) * dg)
        win = cs[:, POOL_BUF + 1:, sl] - cs[:, POOL_BUF + 1 - w:POOL_BUF + 1 - w + T, sl]
        cnt = jnp.minimum(w, pos0 + t + 1).astype(F32)
        outs.append(win / cnt[None, :, None])
    pooled = jnp.concatenate(outs, axis=-1).astype(p.dtype)
    d = (pooled - p).reshape(B, T, GROUPS_B, dg)
    y = jnp.einsum('btgc,gcd->btgd', d, w_grp).reshape(B, T, WIDTH_B) * scale
    return y, xp[:, -POOL_BUF:]


def mla_project(hq, hkv, hkr, pos, q_norm, w_uq, kv_norm, w_uk, q_gain, k_gain):
    q = jnp.einsum('btr,rhd->bthd', rms_norm(hq, q_norm), w_uq)
    q = rms_norm(q, q_gain)
    q_nope = q[..., :QK_NOPE]
    q_rope = rope(q[..., QK_NOPE:], pos)
    c = rms_norm(hkv, kv_norm)
    k_nope = jnp.einsum('btc,chd->bthd', c, w_uk)
    kn32 = k_nope.astype(F32)
    kr32 = hkr.astype(F32)
    ms = (jnp.sum(kn32 * kn32, axis=-1) + jnp.sum(kr32 * kr32, axis=-1)[..., None]) / QK_DIM
    r = lax.rsqrt(ms + EPS).astype(c.dtype)
    kr = rope(hkr * k_gain[QK_NOPE:], pos)
    return q_nope, q_rope, c, k_nope, kr, r


def mla_prompt(q_nope, q_rope, c, k_nope, kr, r, w_uv, k_gain):
    B, S, H, _ = q_nope.shape
    rr = r[..., None]
    k = jnp.concatenate([k_nope * k_gain[:QK_NOPE] * rr, kr[:, :, None, :] * rr], axis=-1)
    q = jnp.concatenate([q_nope, q_rope], axis=-1)
    v = jnp.einsum('bsc,chd->bshd', c, w_uv)
    nb = S // Q_BLOCK
    qb = q.reshape(B, nb, Q_BLOCK, H, QK_DIM).transpose(1, 0, 2, 3, 4)
    kpos = jnp.arange(S)

    def block(args):
        qi, bi = args
        qpos = bi * Q_BLOCK + jnp.arange(Q_BLOCK)
        s = jnp.einsum('bqhd,bkhd->bhqk', qi, k).astype(F32) * ATTN_SCALE
        s = jnp.where(kpos[None, :] <= qpos[:, None], s, -jnp.inf)
        p = jax.nn.softmax(s, axis=-1).astype(v.dtype)
        return jnp.einsum('bhqk,bkhd->bqhd', p, v)

    o = lax.map(block, (qb, jnp.arange(nb)))
    return o.transpose(1, 0, 2, 3, 4).reshape(B, S, H, V_DIM)


def mla_sample(q_nope, q_rope, c, kr, r, cache_c, cache_kr, cache_r, page_table, w_uk, w_uv, k_gain):
    T = q_nope.shape[1]
    past = page_table.shape[1] * cache_c.shape[1]
    q_lat = jnp.einsum('bthd,chd->bthc', q_nope * k_gain[:QK_NOPE], w_uk)
    allowed = jnp.concatenate([jnp.ones((T, past), dtype=bool), jnp.tril(jnp.ones((T, T), dtype=bool))], axis=1)

    def one_seq(args):
        ql, qr, cn, krn, rn, pages = args
        C = jnp.concatenate([cache_c[pages].reshape(past, KV_LORA), cn.astype(cache_c.dtype)], axis=0)
        KR = jnp.concatenate([cache_kr[pages].reshape(past, QK_ROPE), krn.astype(cache_kr.dtype)], axis=0)
        R = jnp.concatenate([cache_r[pages].reshape(past, N_HEADS), rn.astype(cache_r.dtype)], axis=0)
        s = (jnp.einsum('thc,kc->htk', ql, C) + jnp.einsum('thd,kd->htk', qr, KR)).astype(F32)
        s = s * R.astype(F32).T[:, None, :] * ATTN_SCALE
        s = jnp.where(allowed[None], s, -jnp.inf)
        p = jax.nn.softmax(s, axis=-1).astype(C.dtype)
        return jnp.einsum('htk,kc->thc', p, C)

    o_lat = lax.map(one_seq, (q_lat, q_rope, c, kr, r, page_table))
    return jnp.einsum('bthc,chd->bthd', o_lat, w_uv)


def setup_inputs(seed: int = 0) -> dict:
    key = jax.random.key(seed)
    k = jax.random.split(key, 32)
    n_pages = PAST_LEN // PAGE_SIZE
    n_phys = (DEC_BATCH * n_pages * 5) // 4

    def w(i, shape, fan_in):
        return jax.random.normal(k[i], shape, F32) * (fan_in ** -0.5)

    def gain(i, shape):
        return 1.0 + 0.05 * jax.random.normal(k[i], shape, F32)

    page_table = jax.random.permutation(k[6], n_phys)[:DEC_BATCH * n_pages].reshape(DEC_BATCH, n_pages).astype(jnp.int32)
    dg = WIDTH_B // GROUPS_B
    return {
        'x_prompt': jax.random.normal(k[0], (BATCH, SEQ, D_MODEL), F32),
        'x_sample': jax.random.normal(k[1], (DEC_BATCH, DEC_SEQ, D_MODEL), F32),
        'cache_latent': jax.random.normal(k[2], (DEPTH, n_phys, PAGE_SIZE, KV_LORA), F32),
        'cache_krope': jax.random.normal(k[3], (DEPTH, n_phys, PAGE_SIZE, QK_ROPE), F32),
        'cache_kscale': jax.random.uniform(k[4], (DEPTH, n_phys, PAGE_SIZE, N_HEADS), F32, 0.8, 1.25),
        'state_pool': jax.random.normal(k[5], (DEPTH, DEC_BATCH, POOL_BUF, WIDTH_B), F32),
        'page_table': page_table,
        'ffn1_norm': gain(7, (DEPTH, D_MODEL)),
        'ffn1_up': w(8, (DEPTH, D_MODEL, 2 * D_FF), D_MODEL),
        'ffn1_down': w(9, (DEPTH, D_FF, D_MODEL), D_FF),
        'mix_norm': gain(10, (DEPTH, D_MODEL)),
        'w_in': w(11, (DEPTH, D_MODEL, N_IN), D_MODEL),
        'gmlp_v_norm': gain(12, (DEPTH, WIDTH_A)),
        'gmlp_ws': w(13, (DEPTH, GROUPS_A, CHUNK, CHUNK), CHUNK),
        'gmlp_b': 1.0 + 0.1 * jax.random.normal(k[14], (DEPTH, GROUPS_A, CHUNK), F32),
        'pool_w': w(15, (DEPTH, GROUPS_B, dg, dg), dg),
        'pool_scale': gain(16, (DEPTH, WIDTH_B)),
        'mla_q_norm': gain(17, (DEPTH, Q_LORA)),
        'mla_w_uq': w(18, (DEPTH, Q_LORA, N_HEADS, QK_DIM), Q_LORA),
        'mla_kv_norm': gain(19, (DEPTH, KV_LORA)),
        'mla_w_uk': w(20, (DEPTH, KV_LORA, N_HEADS, QK_NOPE), KV_LORA),
        'mla_w_uv': w(21, (DEPTH, KV_LORA, N_HEADS, V_DIM), KV_LORA),
        'mla_q_gain': gain(22, (DEPTH, QK_DIM)),
        'mla_k_gain': gain(23, (DEPTH, QK_DIM)),
        'w_o_a': w(24, (DEPTH, WIDTH_A, D_MODEL), WIDTH_A),
        'w_o_b': w(25, (DEPTH, WIDTH_B, D_MODEL), WIDTH_B),
        'w_o_c': w(26, (DEPTH, N_HEADS * V_DIM, D_MODEL), N_HEADS * V_DIM),
        'w_out': w(27, (DEPTH, D_MODEL, D_MODEL), D_MODEL),
        'ffn2_norm': gain(28, (DEPTH, D_MODEL)),
        'ffn2_up': w(29, (DEPTH, D_MODEL, 2 * D_FF), D_MODEL),
        'ffn2_down': w(30, (DEPTH, D_FF, D_MODEL), D_FF),
    }


def reference(x_prompt, x_sample, cache_latent, cache_krope, cache_kscale, state_pool, page_table,
              ffn1_norm, ffn1_up, ffn1_down, mix_norm, w_in, gmlp_v_norm, gmlp_ws, gmlp_b,
              pool_w, pool_scale, mla_q_norm, mla_w_uq, mla_kv_norm, mla_w_uk, mla_w_uv,
              mla_q_gain, mla_k_gain, w_o_a, w_o_b, w_o_c, w_out, ffn2_norm, ffn2_up, ffn2_down):
    past = page_table.shape[1] * cache_latent.shape[2]

    def layer(x, l, pos0, pool_prev, attend):
        B, T, _ = x.shape
        pos = pos0 + jnp.arange(T)
        x = x + 0.5 * swiglu(rms_norm(x, ffn1_norm[l]), ffn1_up[l], ffn1_down[l])
        h = rms_norm(x, mix_norm[l])
        z = h @ w_in[l]
        ya, v_rows = gmlp_mixer(z[..., :OFF_B], gmlp_v_norm[l], gmlp_ws[l], gmlp_b[l])
        yb, pool_new = pool_mixer(z[..., OFF_B:OFF_Q], pool_prev, pos0, pool_w[l], pool_scale[l])
        q_nope, q_rope, c, k_nope, kr, r = mla_project(
            z[..., OFF_Q:OFF_KV], z[..., OFF_KV:OFF_KR], z[..., OFF_KR:OFF_G], pos,
            mla_q_norm[l], mla_w_uq[l], mla_kv_norm[l], mla_w_uk[l], mla_q_gain[l], mla_k_gain[l])
        yc = attend(q_nope, q_rope, c, k_nope, kr, r).reshape(B, T, N_HEADS * V_DIM)
        gates = jax.nn.sigmoid(z[..., OFF_G:].astype(F32)).astype(x.dtype).reshape(B, T, N_BRANCH, D_MODEL)
        merged = (gates[:, :, 0] * (ya @ w_o_a[l]) + gates[:, :, 1] * (yb @ w_o_b[l])
                  + gates[:, :, 2] * (yc @ w_o_c[l]))
        x = x + merged @ w_out[l]
        x = x + 0.5 * swiglu(rms_norm(x, ffn2_norm[l]), ffn2_up[l], ffn2_down[l])
        return x, (c, kr, r, pool_new, v_rows)

    yp, ys = x_prompt, x_sample
    pool_zero = jnp.zeros((x_prompt.shape[0], POOL_BUF, WIDTH_B), x_prompt.dtype)
    p_st, s_st = [], []
    for l in range(DEPTH):
        yp, st_p = layer(yp, l, 0, pool_zero,
                         lambda qn, qr, c, kn, kr, r: mla_prompt(qn, qr, c, kn, kr, r, mla_w_uv[l], mla_k_gain[l]))
        ys, st_s = layer(ys, l, past, state_pool[l],
                         lambda qn, qr, c, kn, kr, r: mla_sample(qn, qr, c, kr, r, cache_latent[l], cache_krope[l],
                                                                cache_kscale[l], page_table, mla_w_uk[l],
                                                                mla_w_uv[l], mla_k_gain[l]))
        p_st.append(st_p)
        s_st.append(st_s)

    latent_p = jnp.stack([s[0] for s in p_st])
    krope_p = jnp.stack([s[1] for s in p_st])
    kscale_p = jnp.stack([s[2] for s in p_st])
    pool_p = jnp.stack([s[3] for s in p_st])
    latent_s = jnp.stack([s[0] for s in s_st])
    krope_s = jnp.stack([s[1] for s in s_st])
    kscale_s = jnp.stack([s[2] for s in s_st])
    pool_s = jnp.stack([s[3] for s in s_st])
    gmlp_v_s = jnp.stack([s[4] for s in s_st])
    return (yp, ys, latent_p, krope_p, kscale_p, pool_p, latent_s, krope_s, kscale_s, pool_s, gmlp_v_s)
```

```python
import functools

import jax
import jax.numpy as jnp
from jax import lax
from jax.experimental import pallas as pl
from jax.experimental.pallas import tpu as pltpu

F32 = jnp.float32
BF16 = jnp.bfloat16

D_MODEL = 2048
D_FF = 5632
WIDTH_A = 1024
GROUPS_A = 4
CHUNK = 128
WIDTH_B = 1024
POOL_WINDOWS = (2, 4, 8, 16)
POOL_BUF = 15
N_HEADS = 16
QK_NOPE = 128
QK_ROPE = 64
QK_DIM = QK_NOPE + QK_ROPE
V_DIM = 128
Q_LORA = 512
KV_LORA = 512
ROPE_THETA = 10000.0
ATTN_SCALE = QK_DIM ** -0.5
EPS = 1e-6
PAGE_SIZE = 128

LANES = 128
VMEM_BYTES = 64 * 1024 * 1024

Z_A = 0
Z_B = 2 * WIDTH_A
Z_Q = Z_B + WIDTH_B
Z_KV = Z_Q + Q_LORA
Z_G = Z_KV + KV_LORA
Z_KR = Z_G + 3 * D_MODEL
Z_N = Z_KR + LANES
QPAD = 2 * LANES

FFN_TF = 512
IN_BN = 1152
MERGE_BN = 1024
ATT_BQ = 512
DEC_PAGES = 16


def _vmem_limit(block_bytes, scratch_bytes, temp_bytes):
    need = 2 * block_bytes + scratch_bytes + temp_bytes
    return int(min(need, VMEM_BYTES - 4 * 1024 * 1024))


def _nbytes(shape, dtype):
    n = 1
    for s in shape:
        n *= s
    return n * jnp.dtype(dtype).itemsize


def _rms(x, g):
    return x * lax.rsqrt(jnp.mean(x * x, axis=-1, keepdims=True) + EPS) * g


def _ffn_kernel(x_ref, g_ref, wg_ref, wu_ref, wd_ref, o_ref, h_ref, acc_ref, *, nf):
    f = pl.program_id(1)

    @pl.when(f == 0)
    def _():
        h_ref[...] = _rms(x_ref[...], g_ref[...]).astype(BF16)
        acc_ref[...] = jnp.zeros_like(acc_ref)

    h = h_ref[...]
    g = jnp.dot(h, wg_ref[...], preferred_element_type=F32)
    u = jnp.dot(h, wu_ref[...], preferred_element_type=F32)
    a = (g * jax.nn.sigmoid(g) * u).astype(BF16)
    acc_ref[...] += jnp.dot(a, wd_ref[...], preferred_element_type=F32)

    @pl.when(f == nf - 1)
    def _():
        o_ref[...] = x_ref[...] + 0.5 * acc_ref[...]


def _ffn(x, norm_g, w_up, w_down, bm):
    m = x.shape[0]
    nf = D_FF // FFN_TF
    blocks = (2 * _nbytes((bm, D_MODEL), F32) + 2 * _nbytes((D_MODEL, FFN_TF), BF16)
              + _nbytes((FFN_TF, D_MODEL), BF16))
    scratch = _nbytes((bm, D_MODEL), BF16) + _nbytes((bm, D_MODEL), F32)
    temps = 4 * _nbytes((bm, FFN_TF), F32) + _nbytes((bm, D_MODEL), F32)
    return pl.pallas_call(
        functools.partial(_ffn_kernel, nf=nf),
        grid=(m // bm, nf),
        in_specs=[pl.BlockSpec((bm, D_MODEL), lambda i, f: (i, 0)),
                  pl.BlockSpec((1, D_MODEL), lambda i, f: (0, 0)),
                  pl.BlockSpec((D_MODEL, FFN_TF), lambda i, f: (0, f)),
                  pl.BlockSpec((D_MODEL, FFN_TF), lambda i, f: (0, f + nf)),
                  pl.BlockSpec((FFN_TF, D_MODEL), lambda i, f: (f, 0))],
        out_specs=pl.BlockSpec((bm, D_MODEL), lambda i, f: (i, 0)),
        out_shape=jax.ShapeDtypeStruct((m, D_MODEL), F32),
        scratch_shapes=[pltpu.VMEM((bm, D_MODEL), BF16), pltpu.VMEM((bm, D_MODEL), F32)],
        compiler_params=pltpu.CompilerParams(
            dimension_semantics=("parallel", "arbitrary"),
            vmem_limit_bytes=_vmem_limit(blocks, scratch, temps)),
        name="ffn",
    )(x, norm_g, w_up, w_up, w_down)


def _in_proj_kernel(x_ref, g_ref, w_ref, o_ref, h_ref):
    @pl.when(pl.program_id(1) == 0)
    def _():
        h_ref[...] = _rms(x_ref[...], g_ref[...]).astype(BF16)

    o_ref[...] = jnp.dot(h_ref[...], w_ref[...], preferred_element_type=F32)


def _in_proj(x, norm_g, w, bm):
    m = x.shape[0]
    blocks = (_nbytes((bm, D_MODEL), F32) + _nbytes((D_MODEL, IN_BN), BF16)
              + _nbytes((bm, IN_BN), F32))
    scratch = _nbytes((bm, D_MODEL), BF16)
    temps = _nbytes((bm, D_MODEL), F32) + _nbytes((bm, IN_BN), F32)
    return pl.pallas_call(
        _in_proj_kernel,
        grid=(m // bm, Z_N // IN_BN),
        in_specs=[pl.BlockSpec((bm, D_MODEL), lambda i, n: (i, 0)),
                  pl.BlockSpec((1, D_MODEL), lambda i, n: (0, 0)),
                  pl.BlockSpec((D_MODEL, IN_BN), lambda i, n: (0, n))],
        out_specs=pl.BlockSpec((bm, IN_BN), lambda i, n: (i, n)),
        out_shape=jax.ShapeDtypeStruct((m, Z_N), F32),
        scratch_shapes=[pltpu.VMEM((bm, D_MODEL), BF16)],
        compiler_params=pltpu.CompilerParams(
            dimension_semantics=("parallel", "arbitrary"),
            vmem_limit_bytes=_vmem_limit(blocks, scratch, temps)),
        name="in_proj",
    )(x, norm_g, w)


def _gmlp_prompt_kernel(z_ref, vg_ref, ws_ref, b_ref, ya_ref, *, bt):
    z = jax.nn.gelu(z_ref[...])
    u = z[:, :WIDTH_A]
    v = _rms(z[:, WIDTH_A:], vg_ref[...]).astype(BF16)
    dg = WIDTH_A // GROUPS_A
    for c in range(bt // CHUNK):
        rows = slice(c * CHUNK, (c + 1) * CHUNK)
        for g in range(GROUPS_A):
            cols = slice(g * dg, (g + 1) * dg)
            s = jnp.dot(ws_ref[g], v[rows, cols], preferred_element_type=F32) + b_ref[:, g:g + 1]
            ya_ref[rows, cols] = (u[rows, cols] * s).astype(BF16)


def _gmlp_prompt(z, v_gain, ws_tril, b_t, bt):
    m = z.shape[0]
    blocks = _nbytes((bt, 2 * WIDTH_A), F32) + _nbytes((bt, WIDTH_A), BF16) + _nbytes((GROUPS_A, CHUNK, CHUNK), BF16)
    temps = 3 * _nbytes((bt, 2 * WIDTH_A), F32)
    return pl.pallas_call(
        functools.partial(_gmlp_prompt_kernel, bt=bt),
        grid=(m // bt,),
        in_specs=[pl.BlockSpec((bt, 2 * WIDTH_A), lambda i: (i, Z_A // (2 * WIDTH_A))),
                  pl.BlockSpec((1, WIDTH_A), lambda i: (0, 0)),
                  pl.BlockSpec((GROUPS_A, CHUNK, CHUNK), lambda i: (0, 0, 0)),
                  pl.BlockSpec((CHUNK, GROUPS_A), lambda i: (0, 0))],
        out_specs=pl.BlockSpec((bt, WIDTH_A), lambda i: (i, 0)),
        out_shape=jax.ShapeDtypeStruct((m, WIDTH_A), BF16),
        compiler_params=pltpu.CompilerParams(
            dimension_semantics=("parallel",),
            vmem_limit_bytes=_vmem_limit(blocks, 0, temps)),
        name="gmlp_prompt",
    )(z, v_gain, ws_tril, b_t)


def _gmlp_sample_kernel(z_ref, vg_ref, w0_ref, b0_ref, ya_ref, v_ref):
    z = jax.nn.gelu(z_ref[...])
    u = z[:, :WIDTH_A]
    v = _rms(z[:, WIDTH_A:], vg_ref[...])
    v_ref[...] = v
    s = v.astype(BF16).astype(F32) * w0_ref[...] + b0_ref[...]
    ya_ref[...] = (u * s).astype(BF16)


def _gmlp_sample(z, v_gain, w0, b0):
    m = z.shape[0]
    return pl.pallas_call(
        _gmlp_sample_kernel,
        grid=(1,),
        in_specs=[pl.BlockSpec((m, 2 * WIDTH_A), lambda i: (0, Z_A // (2 * WIDTH_A))),
                  pl.BlockSpec((1, WIDTH_A), lambda i: (0, 0)),
                  pl.BlockSpec((1, WIDTH_A), lambda i: (0, 0)),
                  pl.BlockSpec((1, WIDTH_A), lambda i: (0, 0))],
        out_specs=[pl.BlockSpec((m, WIDTH_A), lambda i: (0, 0)),
                   pl.BlockSpec((m, WIDTH_A), lambda i: (0, 0))],
        out_shape=[jax.ShapeDtypeStruct((m, WIDTH_A), BF16),
                   jax.ShapeDtypeStruct((m, WIDTH_A), F32)],
        name="gmlp_sample",
    )(z, v_gain, w0, b0)


HALO = 16


def _pool_prompt_kernel(p_ref, w_ref, sc_ref, yb_ref, new_ref, xp_ref, *, bt, nt):
    t = pl.program_id(1)

    @pl.when(t == 0)
    def _():
        xp_ref[0:HALO, :] = jnp.zeros((HALO, WIDTH_B), F32)

    @pl.when(t > 0)
    def _():
        xp_ref[0:HALO, :] = xp_ref[bt:bt + HALO, :]

    p = p_ref[...]
    xp_ref[HALO:HALO + bt, :] = p
    pos = t * bt + lax.broadcasted_iota(jnp.int32, (bt, 1), 0)
    dg = WIDTH_B // len(POOL_WINDOWS)
    for gi, w in enumerate(POOL_WINDOWS):
        cols = slice(gi * dg, (gi + 1) * dg)
        win = p[:, cols]
        for j in range(1, w):
            win = win + xp_ref[HALO - j:HALO - j + bt, cols]
        cnt = jnp.minimum(w, pos + 1).astype(F32)
        d = (win / cnt - p[:, cols]).astype(BF16)
        y = jnp.dot(d, w_ref[gi], preferred_element_type=F32) * sc_ref[:, cols]
        yb_ref[:, cols] = y.astype(BF16)

    @pl.when(t == nt - 1)
    def _():
        new_ref[0] = xp_ref[HALO + bt - POOL_BUF:HALO + bt, :]


def _pool_prompt(z, pool_w, scale, batch, seq, bt):
    nt = seq // bt
    dg = WIDTH_B // len(POOL_WINDOWS)
    blocks = (_nbytes((bt, WIDTH_B), F32) + _nbytes((len(POOL_WINDOWS), dg, dg), BF16)
              + _nbytes((bt, WIDTH_B), BF16) + _nbytes((16, WIDTH_B), F32))
    scratch = _nbytes((bt + HALO, WIDTH_B), F32)
    temps = 4 * _nbytes((bt, WIDTH_B), F32)
    return pl.pallas_call(
        functools.partial(_pool_prompt_kernel, bt=bt, nt=nt),
        grid=(batch, nt),
        in_specs=[pl.BlockSpec((bt, WIDTH_B), lambda b, t: (b * nt + t, Z_B // WIDTH_B)),
                  pl.BlockSpec((len(POOL_WINDOWS), dg, dg), lambda b, t: (0, 0, 0)),
                  pl.BlockSpec((1, WIDTH_B), lambda b, t: (0, 0))],
        out_specs=[pl.BlockSpec((bt, WIDTH_B), lambda b, t: (b * nt + t, 0)),
                   pl.BlockSpec((1, POOL_BUF, WIDTH_B), lambda b, t: (b, 0, 0))],
        out_shape=[jax.ShapeDtypeStruct((batch * seq, WIDTH_B), BF16),
                   jax.ShapeDtypeStruct((batch, POOL_BUF, WIDTH_B), F32)],
        scratch_shapes=[pltpu.VMEM((bt + HALO, WIDTH_B), F32)],
        compiler_params=pltpu.CompilerParams(
            dimension_semantics=("parallel", "arbitrary"),
            vmem_limit_bytes=_vmem_limit(blocks, scratch, temps)),
        name="pool_prompt",
    )(z, pool_w, scale)


def _pool_sample_kernel(p_ref, prev_ref, w_ref, sc_ref, yb_ref, *, pos0):
    p = p_ref[...]
    dg = WIDTH_B // len(POOL_WINDOWS)
    for gi, w in enumerate(POOL_WINDOWS):
        cols = slice(gi * dg, (gi + 1) * dg)
        win = p[:, cols]
        for j in range(1, w):
            win = win + prev_ref[POOL_BUF - j, :, cols]
        cnt = float(min(w, pos0 + 1))
        d = (win / cnt - p[:, cols]).astype(BF16)
        y = jnp.dot(d, w_ref[gi], preferred_element_type=F32) * sc_ref[:, cols]
        yb_ref[:, cols] = y.astype(BF16)


def _pool_sample(z, prev_t, pool_w, scale, pos0):
    m = z.shape[0]
    dg = WIDTH_B // len(POOL_WINDOWS)
    blocks = (_nbytes((m, WIDTH_B), F32) + _nbytes((POOL_BUF, m, WIDTH_B), F32)
              + _nbytes((len(POOL_WINDOWS), dg, dg), BF16) + _nbytes((m, WIDTH_B), BF16))
    return pl.pallas_call(
        functools.partial(_pool_sample_kernel, pos0=pos0),
        grid=(1,),
        in_specs=[pl.BlockSpec((m, WIDTH_B), lambda i: (0, Z_B // WIDTH_B)),
                  pl.BlockSpec((POOL_BUF, m, WIDTH_B), lambda i: (0, 0, 0)),
                  pl.BlockSpec((len(POOL_WINDOWS), dg, dg), lambda i: (0, 0, 0)),
                  pl.BlockSpec((1, WIDTH_B), lambda i: (0, 0))],
        out_specs=pl.BlockSpec((m, WIDTH_B), lambda i: (0, 0)),
        out_shape=jax.ShapeDtypeStruct((m, WIDTH_B), BF16),
        compiler_params=pltpu.CompilerParams(
            vmem_limit_bytes=_vmem_limit(blocks, 0, 4 * _nbytes((m, WIDTH_B), F32))),
        name="pool_sample",
    )(z, prev_t, pool_w, scale)


def _rope(r, cos_t, sin_t, lane):
    rot = jnp.where(lane < QK_ROPE // 2, -pltpu.roll(r, LANES - QK_ROPE // 2, 1),
                    pltpu.roll(r, QK_ROPE // 2, 1))
    return r * cos_t + rot * sin_t


def _mla_common(hq_ref, hkv_ref, hkr_ref, qn_ref, kvn_ref, kgr_ref, cos_ref, sin_ref,
                c_ref, kr_ref):
    cos_t = cos_ref[...]
    sin_t = sin_ref[...]
    lane = lax.broadcasted_iota(jnp.int32, (1, LANES), 1)
    qn = _rms(hq_ref[...], qn_ref[...]).astype(BF16)
    c = _rms(hkv_ref[...], kvn_ref[...])
    c_ref[...] = c
    hkr = hkr_ref[...]
    krsq = jnp.sum(hkr * hkr, axis=-1, keepdims=True)
    kr = _rope(hkr * kgr_ref[...], cos_t, sin_t, lane)
    kr_ref[...] = kr[:, :QK_ROPE]
    return qn, c.astype(BF16), krsq, kr, cos_t, sin_t, lane


def _head_query(qall, h, qg, cos_t, sin_t, lane):
    qh = qall[:, h * QPAD:(h + 1) * QPAD]
    y = qh * lax.rsqrt(jnp.sum(qh * qh, axis=-1, keepdims=True) / QK_DIM + EPS) * qg
    return y[:, :QK_NOPE], _rope(y[:, QK_NOPE:], cos_t, sin_t, lane)


def _head_rscale(kall, h, krsq):
    kn = kall[:, h * QK_NOPE:(h + 1) * QK_NOPE]
    ms = (jnp.sum(kn * kn, axis=-1, keepdims=True) + krsq) / QK_DIM
    return kn, lax.rsqrt(ms + EPS)


def _mla_prompt_kernel(hq_ref, hkv_ref, hkr_ref, qn_ref, kvn_ref, wq_ref, wuk_ref, wuv_ref,
                       qg_ref, kgn_ref, kgr_ref, cos_ref, sin_ref,
                       q_ref, k_ref, v_ref, c_ref, kr_ref, r_ref):
    qn, cb, krsq, kr, cos_t, sin_t, lane = _mla_common(
        hq_ref, hkv_ref, hkr_ref, qn_ref, kvn_ref, kgr_ref, cos_ref, sin_ref, c_ref, kr_ref)
    qall = jnp.dot(qn, wq_ref[...], preferred_element_type=F32)
    kall = jnp.dot(cb, wuk_ref[...], preferred_element_type=F32)
    v_ref[...] = jnp.dot(cb, wuv_ref[...], preferred_element_type=F32).astype(BF16)
    qg = qg_ref[...]
    kgn = kgn_ref[...]
    racc = jnp.zeros((qn.shape[0], LANES), F32)
    for h in range(N_HEADS):
        q_nope, q_rope = _head_query(qall, h, qg, cos_t, sin_t, lane)
        q_ref[:, h * QPAD:h * QPAD + QK_NOPE] = q_nope.astype(BF16)
        q_ref[:, h * QPAD + QK_NOPE:(h + 1) * QPAD] = q_rope.astype(BF16)
        kn, r = _head_rscale(kall, h, krsq)
        k_ref[:, h * QPAD:h * QPAD + QK_NOPE] = (kn * kgn * r).astype(BF16)
        k_ref[:, h * QPAD + QK_NOPE:(h + 1) * QPAD] = (kr * r).astype(BF16)
        racc = jnp.where(lane == h, r, racc)
    r_ref[...] = racc[:, :N_HEADS]


def _mla_sample_kernel(hq_ref, hkv_ref, hkr_ref, qn_ref, kvn_ref, wq_ref, wuk_ref, wukt_ref,
                       qg_ref, kgn_ref, kgr_ref, cos_ref, sin_ref,
                       ql_ref, qr_ref, c_ref, kr_ref, r_ref):
    qn, cb, krsq, kr, cos_t, sin_t, lane = _mla_common(
        hq_ref, hkv_ref, hkr_ref, qn_ref, kvn_ref, kgr_ref, cos_ref, sin_ref, c_ref, kr_ref)
    qall = jnp.dot(qn, wq_ref[...], preferred_element_type=F32)
    kall = jnp.dot(cb, wuk_ref[...], preferred_element_type=F32)
    qg = qg_ref[...]
    kgn = kgn_ref[...]
    racc = jnp.zeros((qn.shape[0], LANES), F32)
    q_rope_prev = None
    for h in range(N_HEADS):
        q_nope, q_rope = _head_query(qall, h, qg, cos_t, sin_t, lane)
        ql = jnp.dot((q_nope * kgn).astype(BF16), wukt_ref[h], preferred_element_type=F32)
        ql_ref[:, h * KV_LORA:(h + 1) * KV_LORA] = ql.astype(BF16)
        if h % 2 == 0:
            q_rope_prev = q_rope
        else:
            pair = q_rope_prev + pltpu.roll(q_rope, QK_ROPE, 1)
            qr_ref[:, (h - 1) * QK_ROPE:(h + 1) * QK_ROPE] = pair.astype(BF16)
        _, r = _head_rscale(kall, h, krsq)
        racc = jnp.where(lane == h, r, racc)
    r_ref[...] = racc[:, :N_HEADS]


def _mla_in_specs(bt, cos_rows):
    tab = ((lambda i: (i, 0)) if cos_rows else (lambda i: (0, 0)))
    return [pl.BlockSpec((bt, Q_LORA), lambda i: (i, Z_Q // Q_LORA)),
            pl.BlockSpec((bt, KV_LORA), lambda i: (i, Z_KV // KV_LORA)),
            pl.BlockSpec((bt, LANES), lambda i: (i, Z_KR // LANES)),
            pl.BlockSpec((1, Q_LORA), lambda i: (0, 0)),
            pl.BlockSpec((1, KV_LORA), lambda i: (0, 0))], tab


def _mla_prompt(z, lw, cos_t, sin_t, seq, bt):
    m = z.shape[0]
    nt = seq // bt
    specs, _ = _mla_in_specs(bt, True)
    hw = N_HEADS * QPAD
    blocks = (2 * _nbytes((bt, Q_LORA), F32) + _nbytes((Q_LORA, hw), BF16)
              + 2 * _nbytes((KV_LORA, N_HEADS * QK_NOPE), BF16) + 2 * _nbytes((bt, hw), BF16)
              + _nbytes((bt, N_HEADS * V_DIM), BF16) + 4 * _nbytes((bt, KV_LORA), F32))
    temps = 2 * _nbytes((bt, hw), F32) + 2 * _nbytes((bt, N_HEADS * QK_NOPE), F32)
    return pl.pallas_call(
        _mla_prompt_kernel,
        grid=(m // bt,),
        in_specs=specs + [
            pl.BlockSpec((Q_LORA, hw), lambda i: (0, 0)),
            pl.BlockSpec((KV_LORA, N_HEADS * QK_NOPE), lambda i: (0, 0)),
            pl.BlockSpec((KV_LORA, N_HEADS * V_DIM), lambda i: (0, 0)),
            pl.BlockSpec((1, QPAD), lambda i: (0, 0)),
            pl.BlockSpec((1, QK_NOPE), lambda i: (0, 0)),
            pl.BlockSpec((1, LANES), lambda i: (0, 0)),
            pl.BlockSpec((bt, LANES), lambda i: (i % nt, 0)),
            pl.BlockSpec((bt, LANES), lambda i: (i % nt, 0))],
        out_specs=[pl.BlockSpec((bt, hw), lambda i: (i, 0)),
                   pl.BlockSpec((bt, hw), lambda i: (i, 0)),
                   pl.BlockSpec((bt, N_HEADS * V_DIM), lambda i: (i, 0)),
                   pl.BlockSpec((bt, KV_LORA), lambda i: (i, 0)),
                   pl.BlockSpec((bt, QK_ROPE), lambda i: (i, 0)),
                   pl.BlockSpec((bt, N_HEADS), lambda i: (i, 0))],
        out_shape=[jax.ShapeDtypeStruct((m, hw), BF16),
                   jax.ShapeDtypeStruct((m, hw), BF16),
                   jax.ShapeDtypeStruct((m, N_HEADS * V_DIM), BF16),
                   jax.ShapeDtypeStruct((m, KV_LORA), F32),
                   jax.ShapeDtypeStruct((m, QK_ROPE), F32),
                   jax.ShapeDtypeStruct((m, N_HEADS), F32)],
        compiler_params=pltpu.CompilerParams(
            dimension_semantics=("parallel",),
            vmem_limit_bytes=_vmem_limit(blocks, 0, temps)),
        name="mla_prompt_proj",
    )(z, z, z, lw["q_norm"], lw["kv_norm"], lw["w_uq"], lw["w_uk"], lw["w_uv"],
      lw["q_gain"], lw["k_gain_nope"], lw["k_gain_rope"], cos_t, sin_t)


def _mla_sample(z, lw, cos_t, sin_t):
    m = z.shape[0]
    specs, _ = _mla_in_specs(m, False)
    hw = N_HEADS * QPAD
    blocks = (2 * _nbytes((m, Q_LORA), F32) + _nbytes((Q_LORA, hw), BF16)
              + 2 * _nbytes((KV_LORA, N_HEADS * QK_NOPE), BF16)
              + _nbytes((m, N_HEADS * KV_LORA), BF16) + 4 * _nbytes((m, KV_LORA), F32))
    temps = 2 * _nbytes((m, hw), F32) + 2 * _nbytes((m, N_HEADS * QK_NOPE), F32)
    return pl.pallas_call(
        _mla_sample_kernel,
        grid=(1,),
        in_specs=specs + [
            pl.BlockSpec((Q_LORA, hw), lambda i: (0, 0)),
            pl.BlockSpec((KV_LORA, N_HEADS * QK_NOPE), lambda i: (0, 0)),
            pl.BlockSpec((N_HEADS, QK_NOPE, KV_LORA), lambda i: (0, 0, 0)),
            pl.BlockSpec((1, QPAD), lambda i: (0, 0)),
            pl.BlockSpec((1, QK_NOPE), lambda i: (0, 0)),
            pl.BlockSpec((1, LANES), lambda i: (0, 0)),
            pl.BlockSpec((m, LANES), lambda i: (0, 0)),
            pl.BlockSpec((m, LANES), lambda i: (0, 0))],
        out_specs=[pl.BlockSpec((m, N_HEADS * KV_LORA), lambda i: (0, 0)),
                   pl.BlockSpec((m, N_HEADS * QK_ROPE), lambda i: (0, 0)),
                   pl.BlockSpec((m, KV_LORA), lambda i: (0, 0)),
                   pl.BlockSpec((m, QK_ROPE), lambda i: (0, 0)),
                   pl.BlockSpec((m, N_HEADS), lambda i: (0, 0))],
        out_shape=[jax.ShapeDtypeStruct((m, N_HEADS * KV_LORA), BF16),
                   jax.ShapeDtypeStruct((m, N_HEADS * QK_ROPE), BF16),
                   jax.ShapeDtypeStruct((m, KV_LORA), F32),
                   jax.ShapeDtypeStruct((m, QK_ROPE), F32),
                   jax.ShapeDtypeStruct((m, N_HEADS), F32)],
        compiler_params=pltpu.CompilerParams(
            vmem_limit_bytes=_vmem_limit(blocks, 0, temps)),
        name="mla_sample_proj",
    )(z, z, z, lw["q_norm"], lw["kv_norm"], lw["w_uq"], lw["w_uk"], lw["w_ukt"],
      lw["q_gain"], lw["k_gain_nope"], lw["k_gain_rope"], cos_t, sin_t)


def _flash_kernel(q_ref, k_ref, v_ref, o_ref, *, blk):
    qi = pl.program_id(2)
    q = q_ref[0]
    row = qi * blk + lax.broadcasted_iota(jnp.int32, (blk, blk), 0)
    col0 = lax.broadcasted_iota(jnp.int32, (blk, blk), 1)

    def body(j, carry):
        m, l, acc = carry
        off = pl.multiple_of(j * blk, blk)
        kb = k_ref[0, pl.ds(off, blk), :]
        vb = v_ref[0, pl.ds(off, blk), :]
        s = lax.dot_general(q, kb, (((1,), (1,)), ((), ())), preferred_element_type=F32) * ATTN_SCALE
        s = jnp.where(col0 + j * blk <= row, s, -jnp.inf)
        m_new = jnp.maximum(m, jnp.max(s, axis=-1, keepdims=True))
        alpha = jnp.exp(m - m_new)
        p = jnp.exp(s - m_new)
        l = alpha * l + jnp.sum(p, axis=-1, keepdims=True)
        acc = alpha * acc + jnp.dot(p.astype(BF16), vb, preferred_element_type=F32)
        return m_new, l, acc

    m0 = jnp.full((blk, 1), -jnp.inf, F32)
    l0 = jnp.zeros((blk, 1), F32)
    a0 = jnp.zeros((blk, V_DIM), F32)
    _, l, acc = lax.fori_loop(0, qi + 1, body, (m0, l0, a0))
    o_ref[0] = (acc / l).astype(BF16)


def _flash(q, k, v, blk):
    batch, seq, _ = q.shape
    blocks = (_nbytes((blk, QPAD), BF16) + _nbytes((seq, QPAD), BF16) + _nbytes((seq, V_DIM), BF16)
              + _nbytes((blk, V_DIM), BF16))
    temps = 6 * _nbytes((blk, blk), F32)
    return pl.pallas_call(
        functools.partial(_flash_kernel, blk=blk),
        grid=(batch, N_HEADS, seq // blk),
        in_specs=[pl.BlockSpec((1, blk, QPAD), lambda b, h, i: (b, i, h)),
                  pl.BlockSpec((1, seq, QPAD), lambda b, h, i: (b, 0, h)),
                  pl.BlockSpec((1, seq, V_DIM), lambda b, h, i: (b, 0, h))],
        out_specs=pl.BlockSpec((1, blk, V_DIM), lambda b, h, i: (b, i, h)),
        out_shape=jax.ShapeDtypeStruct((batch, seq, N_HEADS * V_DIM), BF16),
        compiler_params=pltpu.CompilerParams(
            dimension_semantics=("parallel", "parallel", "arbitrary"),
            vmem_limit_bytes=_vmem_limit(blocks, 0, temps)),
        name="prompt_attention",
    )(q, k, v)


def _decode_kernel(pt_ref, ql_ref, qr_ref, c_ref, kr_ref, r_ref, cc_hbm, ckr_hbm, crt_hbm,
                   o_ref, cbuf, krbuf, rbuf, sem, m_s, l_s, acc_s, *, layer, n_pages, nch, nsteps):
    i = pl.program_id(0)
    slot = i % 2
    ch = i % nch

    def copies(step, slot_):
        base = (step // nch) * n_pages + (step % nch) * DEC_PAGES
        out = []
        for p in range(DEC_PAGES):
            pg = pt_ref[base + p]
            rows = pl.ds(p * PAGE_SIZE, PAGE_SIZE)
            out.append(pltpu.make_async_copy(cc_hbm.at[layer, pg], cbuf.at[slot_, rows, :], sem.at[slot_, 0]))
            out.append(pltpu.make_async_copy(ckr_hbm.at[layer, pg], krbuf.at[slot_, rows, :], sem.at[slot_, 1]))
            out.append(pltpu.make_async_copy(crt_hbm.at[layer, pg], rbuf.at[slot_, :, rows], sem.at[slot_, 2]))
        return out

    @pl.when(i == 0)
    def _():
        for cp in copies(0, 0):
            cp.start()

    @pl.when(i + 1 < nsteps)
    def _():
        for cp in copies(i + 1, 1 - slot):
            cp.start()

    ql = ql_ref[0]
    qr = qr_ref[0]

    @pl.when(ch == 0)
    def _():
        cn = c_ref[0].astype(BF16).astype(F32)
        krn = kr_ref[0].astype(BF16).astype(F32)
        s0 = (jnp.sum(ql.astype(F32) * cn, axis=-1, keepdims=True)
              + jnp.sum(qr.astype(F32) * krn, axis=-1, keepdims=True))
        s0 = s0 * r_ref[0] * ATTN_SCALE
        m_s[...] = jnp.broadcast_to(s0, m_s.shape)
        l_s[...] = jnp.ones_like(l_s)
        acc_s[...] = jnp.broadcast_to(cn, acc_s.shape)

    for cp in copies(i, slot):
        cp.wait()

    cb = cbuf[slot].astype(BF16)
    krb = krbuf[slot].astype(BF16)
    dn = (((1,), (1,)), ((), ()))
    s = (lax.dot_general(ql, cb, dn, preferred_element_type=F32)
         + lax.dot_general(qr, krb, dn, preferred_element_type=F32))
    s = s * rbuf[slot] * ATTN_SCALE
    m_prev = m_s[:, :1]
    m_new = jnp.maximum(m_prev, jnp.max(s, axis=-1, keepdims=True))
    alpha = jnp.exp(m_prev - m_new)
    p = jnp.exp(s - m_new)
    l_s[...] = alpha * l_s[...] + jnp.sum(p, axis=-1, keepdims=True)
    acc_s[...] = alpha * acc_s[...] + jnp.dot(p.astype(BF16), cb, preferred_element_type=F32)
    m_s[...] = jnp.broadcast_to(m_new, m_s.shape)

    @pl.when(ch == nch - 1)
    def _():
        o_ref[0] = (acc_s[...] / l_s[:, :1]).astype(BF16)


def _decode_attention(page_table, ql, qr, c_new, kr_new, r_new, cache_c, cache_kr, cache_rt, layer):
    nseq, n_pages = page_table.shape
    nch = n_pages // DEC_PAGES
    nsteps = nseq * nch
    keys = DEC_PAGES * PAGE_SIZE
    scratch = (2 * _nbytes((keys, KV_LORA), F32) + 2 * _nbytes((keys, LANES), F32)
               + 2 * _nbytes((N_HEADS, keys), F32) + 2 * _nbytes((N_HEADS, LANES), F32)
               + _nbytes((N_HEADS, KV_LORA), F32))
    blocks = 2 * _nbytes((N_HEADS, KV_LORA), BF16) + 3 * _nbytes((8, KV_LORA), F32)
    temps = 2 * _nbytes((keys, KV_LORA), BF16) + 6 * _nbytes((N_HEADS, keys), F32)
    grid_spec = pltpu.PrefetchScalarGridSpec(
        num_scalar_prefetch=1,
        grid=(nsteps,),
        in_specs=[pl.BlockSpec((1, N_HEADS, KV_LORA), lambda i, pt: (i // nch, 0, 0)),
                  pl.BlockSpec((1, N_HEADS, QK_ROPE), lambda i, pt: (i // nch, 0, 0)),
                  pl.BlockSpec((1, 1, KV_LORA), lambda i, pt: (i // nch, 0, 0)),
                  pl.BlockSpec((1, 1, QK_ROPE), lambda i, pt: (i // nch, 0, 0)),
                  pl.BlockSpec((1, N_HEADS, 1), lambda i, pt: (i // nch, 0, 0)),
                  pl.BlockSpec(memory_space=pl.ANY),
                  pl.BlockSpec(memory_space=pl.ANY),
                  pl.BlockSpec(memory_space=pl.ANY)],
        out_specs=pl.BlockSpec((1, N_HEADS, KV_LORA), lambda i, pt: (i // nch, 0, 0)),
        scratch_shapes=[pltpu.VMEM((2, keys, KV_LORA), F32),
                        pltpu.VMEM((2, keys, QK_ROPE), F32),
                        pltpu.VMEM((2, N_HEADS, keys), F32),
                        pltpu.SemaphoreType.DMA((2, 3)),
                        pltpu.VMEM((N_HEADS, LANES), F32),
                        pltpu.VMEM((N_HEADS, LANES), F32),
                        pltpu.VMEM((N_HEADS, KV_LORA), F32)])
    return pl.pallas_call(
        functools.partial(_decode_kernel, layer=layer, n_pages=n_pages, nch=nch, nsteps=nsteps),
        grid_spec=grid_spec,
        out_shape=jax.ShapeDtypeStruct((nseq, N_HEADS, KV_LORA), BF16),
        compiler_params=pltpu.CompilerParams(
            dimension_semantics=("arbitrary",),
            vmem_limit_bytes=_vmem_limit(blocks, scratch, temps)),
        name="decode_attention",
    )(page_table.reshape(-1), ql, qr, c_new, kr_new, r_new, cache_c, cache_kr, cache_rt)


def _head_out_kernel(x_ref, w_ref, o_ref):
    o_ref[...] = jnp.dot(x_ref[...], w_ref[...], preferred_element_type=F32).astype(BF16)


def _head_out(o_lat, w_uv):
    m = o_lat.shape[0]
    return pl.pallas_call(
        _head_out_kernel,
        grid=(N_HEADS,),
        in_specs=[pl.BlockSpec((m, KV_LORA), lambda h: (0, h)),
                  pl.BlockSpec((KV_LORA, V_DIM), lambda h: (0, h))],
        out_specs=pl.BlockSpec((m, V_DIM), lambda h: (0, h)),
        out_shape=jax.ShapeDtypeStruct((m, N_HEADS * V_DIM), BF16),
        compiler_params=pltpu.CompilerParams(dimension_semantics=("parallel",)),
        name="decode_head_out",
    )(o_lat, w_uv)


def _merge_kernel(ya_ref, yb_ref, yc_ref, g0_ref, g1_ref, g2_ref, wa_ref, wb_ref, wc_ref, o_ref):
    a = jnp.dot(ya_ref[...], wa_ref[...], preferred_element_type=F32)
    b = jnp.dot(yb_ref[...], wb_ref[...], preferred_element_type=F32)
    c = jnp.dot(yc_ref[...], wc_ref[...], preferred_element_type=F32)
    merged = (jax.nn.sigmoid(g0_ref[...]) * a + jax.nn.sigmoid(g1_ref[...]) * b
              + jax.nn.sigmoid(g2_ref[...]) * c)
    o_ref[...] = merged.astype(BF16)


def _merge(ya, yb, yc, z, w_a, w_b, w_c, bm):
    m = ya.shape[0]
    bn = MERGE_BN
    gate = lambda j: pl.BlockSpec((bm, bn), lambda n, i: (i, (Z_G + j * D_MODEL) // bn + n))
    blocks = (2 * _nbytes((bm, WIDTH_A), BF16) + _nbytes((bm, D_MODEL), BF16)
              + 3 * _nbytes((bm, bn), F32) + 2 * _nbytes((WIDTH_A, bn), BF16)
              + _nbytes((D_MODEL, bn), BF16) + _nbytes((bm, bn), BF16))
    temps = 5 * _nbytes((bm, bn), F32)
    return pl.pallas_call(
        _merge_kernel,
        grid=(D_MODEL // bn, m // bm),
        in_specs=[pl.BlockSpec((bm, WIDTH_A), lambda n, i: (i, 0)),
                  pl.BlockSpec((bm, WIDTH_B), lambda n, i: (i, 0)),
                  pl.BlockSpec((bm, N_HEADS * V_DIM), lambda n, i: (i, 0)),
                  gate(0), gate(1), gate(2),
                  pl.BlockSpec((WIDTH_A, bn), lambda n, i: (0, n)),
                  pl.BlockSpec((WIDTH_B, bn), lambda n, i: (0, n)),
                  pl.BlockSpec((N_HEADS * V_DIM, bn), lambda n, i: (0, n))],
        out_specs=pl.BlockSpec((bm, bn), lambda n, i: (i, n)),
        out_shape=jax.ShapeDtypeStruct((m, D_MODEL), BF16),
        compiler_params=pltpu.CompilerParams(
            dimension_semantics=("parallel", "parallel"),
            vmem_limit_bytes=_vmem_limit(blocks, 0, temps)),
        name="merge",
    )(ya, yb, yc, z, z, z, w_a, w_b, w_c)


def _out_proj_kernel(x_ref, m_ref, w_ref, o_ref):
    o_ref[...] = x_ref[...] + jnp.dot(m_ref[...], w_ref[...], preferred_element_type=F32)


def _out_proj(x, merged, w, bm):
    m = x.shape[0]
    blocks = (2 * _nbytes((bm, D_MODEL), F32) + _nbytes((bm, D_MODEL), BF16)
              + _nbytes((D_MODEL, D_MODEL), BF16))
    return pl.pallas_call(
        _out_proj_kernel,
        grid=(m // bm,),
        in_specs=[pl.BlockSpec((bm, D_MODEL), lambda i: (i, 0)),
                  pl.BlockSpec((bm, D_MODEL), lambda i: (i, 0)),
                  pl.BlockSpec((D_MODEL, D_MODEL), lambda i: (0, 0))],
        out_specs=pl.BlockSpec((bm, D_MODEL), lambda i: (i, 0)),
        out_shape=jax.ShapeDtypeStruct((m, D_MODEL), F32),
        compiler_params=pltpu.CompilerParams(
            dimension_semantics=("parallel",),
            vmem_limit_bytes=_vmem_limit(blocks, 0, _nbytes((bm, D_MODEL), F32))),
        name="out_proj",
    )(x, merged, w)


def _rope_tables(pos):
    half = QK_ROPE // 2
    inv = jnp.power(ROPE_THETA, -jnp.arange(half, dtype=F32) / half)
    ang = pos.astype(F32)[:, None] * inv[None, :]
    zeros = jnp.zeros((pos.shape[0], LANES - QK_ROPE), F32)
    cos_t = jnp.concatenate([jnp.cos(ang), jnp.cos(ang), zeros], axis=-1)
    sin_t = jnp.concatenate([jnp.sin(ang), jnp.sin(ang), zeros], axis=-1)
    return cos_t, sin_t


def kernel(x_prompt, x_sample, cache_latent, cache_krope, cache_kscale, state_pool, page_table, ffn1_norm, ffn1_up, ffn1_down, mix_norm, w_in, gmlp_v_norm, gmlp_ws, gmlp_b, pool_w, pool_scale, mla_q_norm, mla_w_uq, mla_kv_norm, mla_w_uk, mla_w_uv, mla_q_gain, mla_k_gain, w_o_a, w_o_b, w_o_c, w_out, ffn2_norm, ffn2_up, ffn2_down):
    batch, seq, _ = x_prompt.shape
    nseq = x_sample.shape[0]
    depth = w_in.shape[0]
    n_pages = page_table.shape[1]
    past = n_pages * cache_latent.shape[2]
    off_kr = Z_G
    off_g = off_kr + QK_ROPE

    cos_p, sin_p = _rope_tables(jnp.arange(seq))
    cos_s, sin_s = _rope_tables(jnp.full((nseq,), past))
    cache_rt = jnp.swapaxes(cache_kscale, 2, 3)
    tril = jnp.tril(jnp.ones((CHUNK, CHUNK), dtype=bool))
    dga = WIDTH_A // GROUPS_A

    yp = x_prompt.reshape(batch * seq, D_MODEL)
    ys = x_sample.reshape(nseq, D_MODEL)
    outs = {k: [] for k in ("lat_p", "kr_p", "r_p", "pool_p", "lat_s", "kr_s", "r_s", "pool_s", "v_s")}
    row = lambda a: a.reshape(1, -1)

    for l in range(depth):
        w_in_l = jnp.concatenate(
            [w_in[l][:, :off_kr], w_in[l][:, off_g:], w_in[l][:, off_kr:off_g],
             jnp.zeros((D_MODEL, LANES - QK_ROPE), F32)], axis=-1).astype(BF16)
        wq = jnp.pad(mla_w_uq[l], ((0, 0), (0, 0), (0, QPAD - QK_DIM))).reshape(Q_LORA, N_HEADS * QPAD).astype(BF16)
        lw = {
            "q_norm": row(mla_q_norm[l]), "kv_norm": row(mla_kv_norm[l]),
            "w_uq": wq,
            "w_uk": mla_w_uk[l].reshape(KV_LORA, N_HEADS * QK_NOPE).astype(BF16),
            "w_ukt": jnp.transpose(mla_w_uk[l], (1, 2, 0)).astype(BF16),
            "w_uv": mla_w_uv[l].reshape(KV_LORA, N_HEADS * V_DIM).astype(BF16),
            "q_gain": row(jnp.pad(mla_q_gain[l], (0, QPAD - QK_DIM))),
            "k_gain_nope": row(mla_k_gain[l][:QK_NOPE]),
            "k_gain_rope": row(jnp.pad(mla_k_gain[l][QK_NOPE:], (0, LANES - QK_ROPE))),
        }
        up1, down1 = ffn1_up[l].astype(BF16), ffn1_down[l].astype(BF16)
        up2, down2 = ffn2_up[l].astype(BF16), ffn2_down[l].astype(BF16)
        woa, wob, woc, wout = (w_o_a[l].astype(BF16), w_o_b[l].astype(BF16),
                               w_o_c[l].astype(BF16), w_out[l].astype(BF16))
        ws_tril = jnp.where(tril, gmlp_ws[l], 0).astype(BF16)
        pw = pool_w[l].astype(BF16)

        yp = _ffn(yp, row(ffn1_norm[l]), up1, down1, 512)
        z = _in_proj(yp, row(mix_norm[l]), w_in_l, 512)
        ya = _gmlp_prompt(z, row(gmlp_v_norm[l]), ws_tril, gmlp_b[l].T, 512)
        yb, pool_new = _pool_prompt(z, pw, row(pool_scale[l]), batch, seq, 512)
        q, k, v, c, kr, r = _mla_prompt(z, lw, cos_p, sin_p, seq, 256)
        yc = _flash(q.reshape(batch, seq, -1), k.reshape(batch, seq, -1), v.reshape(batch, seq, -1), ATT_BQ)
        merged = _merge(ya, yb, yc.reshape(batch * seq, -1), z, woa, wob, woc, 512)
        yp = _out_proj(yp, merged, wout, 512)
        yp = _ffn(yp, row(ffn2_norm[l]), up2, down2, 512)
        outs["lat_p"].append(c.reshape(batch, seq, KV_LORA))
        outs["kr_p"].append(kr.reshape(batch, seq, QK_ROPE))
        outs["r_p"].append(r.reshape(batch, seq, N_HEADS))
        outs["pool_p"].append(pool_new)

        ys = _ffn(ys, row(ffn1_norm[l]), up1, down1, nseq)
        zs = _in_proj(ys, row(mix_norm[l]), w_in_l, nseq)
        w0 = jnp.repeat(gmlp_ws[l][:, 0, 0].astype(BF16).astype(F32), dga)
        b0 = jnp.repeat(gmlp_b[l][:, 0], dga)
        ya_s, v_s = _gmlp_sample(zs, row(gmlp_v_norm[l]), row(w0), row(b0))
        yb_s = _pool_sample(zs, jnp.swapaxes(state_pool[l], 0, 1), pw, row(pool_scale[l]), past)
        ql, qr, c_s, kr_s, r_s = _mla_sample(zs, lw, cos_s, sin_s)
        o_lat = _decode_attention(
            page_table, ql.reshape(nseq, N_HEADS, KV_LORA), qr.reshape(nseq, N_HEADS, QK_ROPE),
            c_s.reshape(nseq, 1, KV_LORA), kr_s.reshape(nseq, 1, QK_ROPE), r_s.reshape(nseq, N_HEADS, 1),
            cache_latent, cache_krope, cache_rt, l)
        yc_s = _head_out(o_lat.reshape(nseq, N_HEADS * KV_LORA), lw["w_uv"])
        merged_s = _merge(ya_s, yb_s, yc_s, zs, woa, wob, woc, nseq)
        ys = _out_proj(ys, merged_s, wout, nseq)
        ys = _ffn(ys, row(ffn2_norm[l]), up2, down2, nseq)
        outs["lat_s"].append(c_s.reshape(nseq, 1, KV_LORA))
        outs["kr_s"].append(kr_s.reshape(nseq, 1, QK_ROPE))
        outs["r_s"].append(r_s.reshape(nseq, 1, N_HEADS))
        outs["pool_s"].append(jnp.concatenate(
            [state_pool[l][:, 1:], zs[:, Z_B:Z_B + WIDTH_B][:, None, :]], axis=1))
        outs["v_s"].append(v_s.reshape(nseq, 1, WIDTH_A))

    st = lambda key: jnp.stack(outs[key])
    return (yp.reshape(batch, seq, D_MODEL), ys.reshape(nseq, 1, D_MODEL),
            st("lat_p"), st("kr_p"), st("r_p"), st("pool_p"),
            st("lat_s"), st("kr_s"), st("r_s"), st("pool_s"), st("v_s"))
```

```python
import functools

import jax
import jax.numpy as jnp
from jax import lax
from jax.experimental import pallas as pl
from jax.experimental.pallas import tpu as pltpu

F32 = jnp.float32
BF16 = jnp.bfloat16

D_MODEL = 2048
D_FF = 5632
WIDTH_A = 1024
GROUPS_A = 4
CHUNK = 128
WIDTH_B = 1024
POOL_WINDOWS = (2, 4, 8, 16)
POOL_BUF = 15
N_HEADS = 16
QK_NOPE = 128
QK_ROPE = 64
QK_DIM = QK_NOPE + QK_ROPE
V_DIM = 128
Q_LORA = 512
KV_LORA = 512
ROPE_THETA = 10000.0
ATTN_SCALE = QK_DIM ** -0.5
LOG2_E = 1.4426950408889634
EPS = 1e-6
PAGE_SIZE = 128

LANES = 128
VMEM_BYTES = 64 * 1024 * 1024

Z_A = 0
Z_B = 2 * WIDTH_A
Z_Q = Z_B + WIDTH_B
Z_KV = Z_Q + Q_LORA
Z_KR = Z_KV + KV_LORA
Z_MIX = Z_KR + LANES
Z_GATES = 3 * D_MODEL
QPAD = 2 * LANES

ROW_TILE = 512
FFN_TF = 512
IN_BM = 1024
IN_BN_MIX = 1408
IN_BN_GATES = 1024
MLA_BT = 256
MERGE_BN = 1024
ATT_BLK = 512
DEC_PAGES = 32
KEY_GROUP = 8


def _vmem_limit(block_bytes, scratch_bytes, temp_bytes):
    need = 2 * block_bytes + scratch_bytes + temp_bytes
    return int(min(need, VMEM_BYTES - 4 * 1024 * 1024))


def _nbytes(shape, dtype):
    n = 1
    for s in shape:
        n *= s
    return n * jnp.dtype(dtype).itemsize


def _rms(x, g):
    return x * lax.rsqrt(jnp.mean(x * x, axis=-1, keepdims=True) + EPS) * g


def _ffn_kernel(x_ref, g_ref, wg_ref, wu_ref, wd_ref, o_ref, h_ref, acc_ref, *, nf):
    f = pl.program_id(1)

    @pl.when(f == 0)
    def _():
        h_ref[...] = _rms(x_ref[...], g_ref[...]).astype(BF16)
        acc_ref[...] = jnp.zeros_like(acc_ref)

    h = h_ref[...]
    g = jnp.dot(h, wg_ref[...], preferred_element_type=F32)
    u = jnp.dot(h, wu_ref[...], preferred_element_type=F32)
    a = (g * jax.nn.sigmoid(g) * u).astype(BF16)
    acc_ref[...] += jnp.dot(a, wd_ref[...], preferred_element_type=F32)

    @pl.when(f == nf - 1)
    def _():
        o_ref[...] = x_ref[...] + 0.5 * acc_ref[...]


def _ffn(x, norm_g, w_up, w_down, layer, bm):
    m = x.shape[0]
    nf = D_FF // FFN_TF
    blocks = (2 * _nbytes((bm, D_MODEL), F32) + 2 * _nbytes((D_MODEL, FFN_TF), BF16)
              + _nbytes((FFN_TF, D_MODEL), BF16))
    scratch = _nbytes((bm, D_MODEL), BF16) + _nbytes((bm, D_MODEL), F32)
    temps = 4 * _nbytes((bm, FFN_TF), F32) + _nbytes((bm, D_MODEL), F32)
    return pl.pallas_call(
        functools.partial(_ffn_kernel, nf=nf),
        grid=(m // bm, nf),
        in_specs=[pl.BlockSpec((bm, D_MODEL), lambda i, f: (i, 0)),
                  pl.BlockSpec((1, D_MODEL), lambda i, f: (0, 0)),
                  pl.BlockSpec((None, D_MODEL, FFN_TF), lambda i, f: (layer, 0, f)),
                  pl.BlockSpec((None, D_MODEL, FFN_TF), lambda i, f: (layer, 0, f + nf)),
                  pl.BlockSpec((None, FFN_TF, D_MODEL), lambda i, f: (layer, f, 0))],
        out_specs=pl.BlockSpec((bm, D_MODEL), lambda i, f: (i, 0)),
        out_shape=jax.ShapeDtypeStruct((m, D_MODEL), F32),
        scratch_shapes=[pltpu.VMEM((bm, D_MODEL), BF16), pltpu.VMEM((bm, D_MODEL), F32)],
        compiler_params=pltpu.CompilerParams(
            dimension_semantics=("parallel", "arbitrary"),
            vmem_limit_bytes=_vmem_limit(blocks, scratch, temps)),
        name="ffn",
    )(x, norm_g, w_up, w_up, w_down)


def _in_proj_kernel(x_ref, g_ref, w_ref, o_ref, h_ref):
    @pl.when(pl.program_id(1) == 0)
    def _():
        h_ref[...] = _rms(x_ref[...], g_ref[...]).astype(BF16)

    o_ref[...] = jnp.dot(h_ref[...], w_ref[...], preferred_element_type=F32)


def _in_proj(x, norm_g, w, layer, bm, bn):
    m = x.shape[0]
    n_out = w.shape[2]
    blocks = (_nbytes((bm, D_MODEL), F32) + _nbytes((D_MODEL, bn), BF16)
              + _nbytes((bm, bn), F32))
    scratch = _nbytes((bm, D_MODEL), BF16)
    temps = _nbytes((bm, D_MODEL), F32) + _nbytes((bm, bn), F32)
    return pl.pallas_call(
        _in_proj_kernel,
        grid=(m // bm, n_out // bn),
        in_specs=[pl.BlockSpec((bm, D_MODEL), lambda i, n: (i, 0)),
                  pl.BlockSpec((1, D_MODEL), lambda i, n: (0, 0)),
                  pl.BlockSpec((None, D_MODEL, bn), lambda i, n: (layer, 0, n))],
        out_specs=pl.BlockSpec((bm, bn), lambda i, n: (i, n)),
        out_shape=jax.ShapeDtypeStruct((m, n_out), F32),
        scratch_shapes=[pltpu.VMEM((bm, D_MODEL), BF16)],
        compiler_params=pltpu.CompilerParams(
            dimension_semantics=("parallel", "arbitrary"),
            vmem_limit_bytes=_vmem_limit(blocks, scratch, temps)),
        name="in_proj",
    )(x, norm_g, w)


def _gmlp_prompt_kernel(z_ref, vg_ref, ws_ref, b_ref, ya_ref, *, bt):
    z = jax.nn.gelu(z_ref[...])
    u = z[:, :WIDTH_A]
    v = _rms(z[:, WIDTH_A:], vg_ref[...]).astype(BF16)
    dg = WIDTH_A // GROUPS_A
    for c in range(bt // CHUNK):
        rows = slice(c * CHUNK, (c + 1) * CHUNK)
        for g in range(GROUPS_A):
            cols = slice(g * dg, (g + 1) * dg)
            s = jnp.dot(ws_ref[g], v[rows, cols], preferred_element_type=F32) + b_ref[:, g:g + 1]
            ya_ref[rows, cols] = (u[rows, cols] * s).astype(BF16)


def _gmlp_prompt(z, v_gain, ws_tril, b_t, bt):
    m = z.shape[0]
    blocks = _nbytes((bt, 2 * WIDTH_A), F32) + _nbytes((bt, WIDTH_A), BF16) + _nbytes((GROUPS_A, CHUNK, CHUNK), BF16)
    temps = 3 * _nbytes((bt, 2 * WIDTH_A), F32)
    return pl.pallas_call(
        functools.partial(_gmlp_prompt_kernel, bt=bt),
        grid=(m // bt,),
        in_specs=[pl.BlockSpec((bt, 2 * WIDTH_A), lambda i: (i, Z_A // (2 * WIDTH_A))),
                  pl.BlockSpec((1, WIDTH_A), lambda i: (0, 0)),
                  pl.BlockSpec((GROUPS_A, CHUNK, CHUNK), lambda i: (0, 0, 0)),
                  pl.BlockSpec((CHUNK, GROUPS_A), lambda i: (0, 0))],
        out_specs=pl.BlockSpec((bt, WIDTH_A), lambda i: (i, 0)),
        out_shape=jax.ShapeDtypeStruct((m, WIDTH_A), BF16),
        compiler_params=pltpu.CompilerParams(
            dimension_semantics=("parallel",),
            vmem_limit_bytes=_vmem_limit(blocks, 0, temps)),
        name="gmlp_prompt",
    )(z, v_gain, ws_tril, b_t)


def _gmlp_sample_kernel(z_ref, vg_ref, w0_ref, b0_ref, ya_ref, v_ref):
    z = jax.nn.gelu(z_ref[...])
    u = z[:, :WIDTH_A]
    v = _rms(z[:, WIDTH_A:], vg_ref[...])
    v_ref[...] = v
    s = v.astype(BF16).astype(F32) * w0_ref[...] + b0_ref[...]
    ya_ref[...] = (u * s).astype(BF16)


def _gmlp_sample(z, v_gain, w0, b0):
    m = z.shape[0]
    return pl.pallas_call(
        _gmlp_sample_kernel,
        grid=(1,),
        in_specs=[pl.BlockSpec((m, 2 * WIDTH_A), lambda i: (0, Z_A // (2 * WIDTH_A))),
                  pl.BlockSpec((1, WIDTH_A), lambda i: (0, 0)),
                  pl.BlockSpec((1, WIDTH_A), lambda i: (0, 0)),
                  pl.BlockSpec((1, WIDTH_A), lambda i: (0, 0))],
        out_specs=[pl.BlockSpec((m, WIDTH_A), lambda i: (0, 0)),
                   pl.BlockSpec((m, WIDTH_A), lambda i: (0, 0))],
        out_shape=[jax.ShapeDtypeStruct((m, WIDTH_A), BF16),
                   jax.ShapeDtypeStruct((m, WIDTH_A), F32)],
        name="gmlp_sample",
    )(z, v_gain, w0, b0)


HALO = 16


def _pool_prompt_kernel(p_ref, w_ref, sc_ref, yb_ref, new_ref, xp_ref, *, bt, nt):
    t = pl.program_id(1)

    @pl.when(t == 0)
    def _():
        xp_ref[0:HALO, :] = jnp.zeros((HALO, WIDTH_B), F32)

    @pl.when(t > 0)
    def _():
        xp_ref[0:HALO, :] = xp_ref[bt:bt + HALO, :]

    p = p_ref[...]
    xp_ref[HALO:HALO + bt, :] = p
    pos = t * bt + lax.broadcasted_iota(jnp.int32, (bt, 1), 0)
    dg = WIDTH_B // len(POOL_WINDOWS)
    for gi, w in enumerate(POOL_WINDOWS):
        cols = slice(gi * dg, (gi + 1) * dg)
        win = p[:, cols]
        for j in range(1, w):
            win = win + xp_ref[HALO - j:HALO - j + bt, cols]
        cnt = jnp.minimum(w, pos + 1).astype(F32)
        d = (win / cnt - p[:, cols]).astype(BF16)
        y = jnp.dot(d, w_ref[gi], preferred_element_type=F32) * sc_ref[:, cols]
        yb_ref[:, cols] = y.astype(BF16)

    @pl.when(t == nt - 1)
    def _():
        new_ref[0] = xp_ref[HALO + bt - POOL_BUF:HALO + bt, :]


def _pool_prompt(z, pool_w, scale, batch, seq, bt):
    nt = seq // bt
    dg = WIDTH_B // len(POOL_WINDOWS)
    blocks = (_nbytes((bt, WIDTH_B), F32) + _nbytes((len(POOL_WINDOWS), dg, dg), BF16)
              + _nbytes((bt, WIDTH_B), BF16) + _nbytes((16, WIDTH_B), F32))
    scratch = _nbytes((bt + HALO, WIDTH_B), F32)
    temps = 4 * _nbytes((bt, WIDTH_B), F32)
    return pl.pallas_call(
        functools.partial(_pool_prompt_kernel, bt=bt, nt=nt),
        grid=(batch, nt),
        in_specs=[pl.BlockSpec((bt, WIDTH_B), lambda b, t: (b * nt + t, Z_B // WIDTH_B)),
                  pl.BlockSpec((len(POOL_WINDOWS), dg, dg), lambda b, t: (0, 0, 0)),
                  pl.BlockSpec((1, WIDTH_B), lambda b, t: (0, 0))],
        out_specs=[pl.BlockSpec((bt, WIDTH_B), lambda b, t: (b * nt + t, 0)),
                   pl.BlockSpec((1, POOL_BUF, WIDTH_B), lambda b, t: (b, 0, 0))],
        out_shape=[jax.ShapeDtypeStruct((batch * seq, WIDTH_B), BF16),
                   jax.ShapeDtypeStruct((batch, POOL_BUF, WIDTH_B), F32)],
        scratch_shapes=[pltpu.VMEM((bt + HALO, WIDTH_B), F32)],
        compiler_params=pltpu.CompilerParams(
            dimension_semantics=("parallel", "arbitrary"),
            vmem_limit_bytes=_vmem_limit(blocks, scratch, temps)),
        name="pool_prompt",
    )(z, pool_w, scale)


def _pool_sample_kernel(p_ref, prev_ref, w_ref, sc_ref, yb_ref, *, pos0):
    p = p_ref[...]
    dg = WIDTH_B // len(POOL_WINDOWS)
    for gi, w in enumerate(POOL_WINDOWS):
        cols = slice(gi * dg, (gi + 1) * dg)
        win = p[:, cols]
        for j in range(1, w):
            win = win + prev_ref[POOL_BUF - j, :, cols]
        cnt = float(min(w, pos0 + 1))
        d = (win / cnt - p[:, cols]).astype(BF16)
        y = jnp.dot(d, w_ref[gi], preferred_element_type=F32) * sc_ref[:, cols]
        yb_ref[:, cols] = y.astype(BF16)


def _pool_sample(z, prev_t, pool_w, scale, pos0):
    m = z.shape[0]
    dg = WIDTH_B // len(POOL_WINDOWS)
    blocks = (_nbytes((m, WIDTH_B), F32) + _nbytes((POOL_BUF, m, WIDTH_B), F32)
              + _nbytes((len(POOL_WINDOWS), dg, dg), BF16) + _nbytes((m, WIDTH_B), BF16))
    return pl.pallas_call(
        functools.partial(_pool_sample_kernel, pos0=pos0),
        grid=(1,),
        in_specs=[pl.BlockSpec((m, WIDTH_B), lambda i: (0, Z_B // WIDTH_B)),
                  pl.BlockSpec((POOL_BUF, m, WIDTH_B), lambda i: (0, 0, 0)),
                  pl.BlockSpec((len(POOL_WINDOWS), dg, dg), lambda i: (0, 0, 0)),
                  pl.BlockSpec((1, WIDTH_B), lambda i: (0, 0))],
        out_specs=pl.BlockSpec((m, WIDTH_B), lambda i: (0, 0)),
        out_shape=jax.ShapeDtypeStruct((m, WIDTH_B), BF16),
        compiler_params=pltpu.CompilerParams(
            vmem_limit_bytes=_vmem_limit(blocks, 0, 4 * _nbytes((m, WIDTH_B), F32))),
        name="pool_sample",
    )(z, prev_t, pool_w, scale)


def _rope(r, cos_t, sin_t, lane):
    rot = jnp.where(lane < QK_ROPE // 2, -pltpu.roll(r, LANES - QK_ROPE // 2, 1),
                    pltpu.roll(r, QK_ROPE // 2, 1))
    return r * cos_t + rot * sin_t


def _mla_common(hq_ref, hkv_ref, hkr_ref, qn_ref, kvn_ref, kgr_ref, cos_ref, sin_ref,
                c_ref, kr_ref):
    cos_t = cos_ref[...]
    sin_t = sin_ref[...]
    lane = lax.broadcasted_iota(jnp.int32, (1, LANES), 1)
    qn = _rms(hq_ref[...], qn_ref[...]).astype(BF16)
    c = _rms(hkv_ref[...], kvn_ref[...])
    c_ref[...] = c
    hkr = hkr_ref[...]
    krsq = jnp.sum(hkr * hkr, axis=-1, keepdims=True)
    kr = _rope(hkr * kgr_ref[...], cos_t, sin_t, lane)
    kr_ref[...] = kr[:, :QK_ROPE]
    return qn, c.astype(BF16), krsq, kr, cos_t, sin_t, lane


def _head_query(qall, h, qg, cos_t, sin_t, lane):
    qh = qall[:, h * QPAD:(h + 1) * QPAD]
    y = qh * lax.rsqrt(jnp.sum(qh * qh, axis=-1, keepdims=True) / QK_DIM + EPS) * qg
    return y[:, :QK_NOPE], _rope(y[:, QK_NOPE:], cos_t, sin_t, lane)


def _head_rscale(kall, h, krsq):
    kn = kall[:, h * QK_NOPE:(h + 1) * QK_NOPE]
    ms = (jnp.sum(kn * kn, axis=-1, keepdims=True) + krsq) / QK_DIM
    return kn, lax.rsqrt(ms + EPS)


def _mla_prompt_kernel(hq_ref, hkv_ref, hkr_ref, qn_ref, kvn_ref, wq_ref, wuk_ref, wuv_ref,
                       qg_ref, kgn_ref, kgr_ref, cos_ref, sin_ref,
                       q_ref, k_ref, v_ref, c_ref, kr_ref, r_ref):
    qn, cb, krsq, kr, cos_t, sin_t, lane = _mla_common(
        hq_ref, hkv_ref, hkr_ref, qn_ref, kvn_ref, kgr_ref, cos_ref, sin_ref, c_ref, kr_ref)
    qall = jnp.dot(qn, wq_ref[...], preferred_element_type=F32)
    kall = jnp.dot(cb, wuk_ref[...], preferred_element_type=F32)
    v_ref[...] = jnp.dot(cb, wuv_ref[...], preferred_element_type=F32).astype(BF16)
    qg = qg_ref[...]
    kgn = kgn_ref[...]
    racc = jnp.zeros((qn.shape[0], LANES), F32)
    for h in range(N_HEADS):
        q_nope, q_rope = _head_query(qall, h, qg, cos_t, sin_t, lane)
        q_ref[:, h * QPAD:h * QPAD + QK_NOPE] = q_nope.astype(BF16)
        q_ref[:, h * QPAD + QK_NOPE:(h + 1) * QPAD] = q_rope.astype(BF16)
        kn, r = _head_rscale(kall, h, krsq)
        k_ref[:, h * QPAD:h * QPAD + QK_NOPE] = (kn * kgn * r).astype(BF16)
        k_ref[:, h * QPAD + QK_NOPE:(h + 1) * QPAD] = (kr * r).astype(BF16)
        racc = jnp.where(lane == h, r, racc)
    r_ref[...] = racc[:, :N_HEADS]


def _mla_sample_kernel(hq_ref, hkv_ref, hkr_ref, qn_ref, kvn_ref, wq_ref, wuk_ref, wukt_ref,
                       qg_ref, kgn_ref, kgr_ref, cos_ref, sin_ref,
                       ql_ref, qr_ref, c_ref, kr_ref, r_ref):
    qn, cb, krsq, kr, cos_t, sin_t, lane = _mla_common(
        hq_ref, hkv_ref, hkr_ref, qn_ref, kvn_ref, kgr_ref, cos_ref, sin_ref, c_ref, kr_ref)
    qall = jnp.dot(qn, wq_ref[...], preferred_element_type=F32)
    kall = jnp.dot(cb, wuk_ref[...], preferred_element_type=F32)
    qg = qg_ref[...]
    kgn = kgn_ref[...]
    racc = jnp.zeros((qn.shape[0], LANES), F32)
    q_rope_prev = None
    for h in range(N_HEADS):
        q_nope, q_rope = _head_query(qall, h, qg, cos_t, sin_t, lane)
        ql = jnp.dot((q_nope * kgn).astype(BF16), wukt_ref[h], preferred_element_type=F32)
        ql_ref[:, h * KV_LORA:(h + 1) * KV_LORA] = ql.astype(BF16)
        if h % 2 == 0:
            q_rope_prev = q_rope
        else:
            pair = q_rope_prev + pltpu.roll(q_rope, QK_ROPE, 1)
            qr_ref[:, (h - 1) * QK_ROPE:(h + 1) * QK_ROPE] = pair.astype(BF16)
        _, r = _head_rscale(kall, h, krsq)
        racc = jnp.where(lane == h, r, racc)
    r_ref[...] = racc[:, :N_HEADS]


def _mla_in_specs(bt, cos_rows):
    tab = ((lambda i: (i, 0)) if cos_rows else (lambda i: (0, 0)))
    return [pl.BlockSpec((bt, Q_LORA), lambda i: (i, Z_Q // Q_LORA)),
            pl.BlockSpec((bt, KV_LORA), lambda i: (i, Z_KV // KV_LORA)),
            pl.BlockSpec((bt, LANES), lambda i: (i, Z_KR // LANES)),
            pl.BlockSpec((1, Q_LORA), lambda i: (0, 0)),
            pl.BlockSpec((1, KV_LORA), lambda i: (0, 0))], tab


def _mla_prompt(z, lw, cos_t, sin_t, seq, bt):
    m = z.shape[0]
    nt = seq // bt
    specs, _ = _mla_in_specs(bt, True)
    hw = N_HEADS * QPAD
    blocks = (2 * _nbytes((bt, Q_LORA), F32) + _nbytes((Q_LORA, hw), BF16)
              + 2 * _nbytes((KV_LORA, N_HEADS * QK_NOPE), BF16) + 2 * _nbytes((bt, hw), BF16)
              + _nbytes((bt, N_HEADS * V_DIM), BF16) + 4 * _nbytes((bt, KV_LORA), F32))
    temps = 2 * _nbytes((bt, hw), F32) + 2 * _nbytes((bt, N_HEADS * QK_NOPE), F32)
    return pl.pallas_call(
        _mla_prompt_kernel,
        grid=(m // bt,),
        in_specs=specs + [
            pl.BlockSpec((Q_LORA, hw), lambda i: (0, 0)),
            pl.BlockSpec((KV_LORA, N_HEADS * QK_NOPE), lambda i: (0, 0)),
            pl.BlockSpec((KV_LORA, N_HEADS * V_DIM), lambda i: (0, 0)),
            pl.BlockSpec((1, QPAD), lambda i: (0, 0)),
            pl.BlockSpec((1, QK_NOPE), lambda i: (0, 0)),
            pl.BlockSpec((1, LANES), lambda i: (0, 0)),
            pl.BlockSpec((bt, LANES), lambda i: (i % nt, 0)),
            pl.BlockSpec((bt, LANES), lambda i: (i % nt, 0))],
        out_specs=[pl.BlockSpec((bt, hw), lambda i: (i, 0)),
                   pl.BlockSpec((bt, hw), lambda i: (i, 0)),
                   pl.BlockSpec((bt, N_HEADS * V_DIM), lambda i: (i, 0)),
                   pl.BlockSpec((bt, KV_LORA), lambda i: (i, 0)),
                   pl.BlockSpec((bt, QK_ROPE), lambda i: (i, 0)),
                   pl.BlockSpec((bt, N_HEADS), lambda i: (i, 0))],
        out_shape=[jax.ShapeDtypeStruct((m, hw), BF16),
                   jax.ShapeDtypeStruct((m, hw), BF16),
                   jax.ShapeDtypeStruct((m, N_HEADS * V_DIM), BF16),
                   jax.ShapeDtypeStruct((m, KV_LORA), F32),
                   jax.ShapeDtypeStruct((m, QK_ROPE), F32),
                   jax.ShapeDtypeStruct((m, N_HEADS), F32)],
        compiler_params=pltpu.CompilerParams(
            dimension_semantics=("parallel",),
            vmem_limit_bytes=_vmem_limit(blocks, 0, temps)),
        name="mla_prompt_proj",
    )(z, z, z, lw["q_norm"], lw["kv_norm"], lw["w_uq"], lw["w_uk"], lw["w_uv"],
      lw["q_gain"], lw["k_gain_nope"], lw["k_gain_rope"], cos_t, sin_t)


def _mla_sample(z, lw, cos_t, sin_t):
    m = z.shape[0]
    specs, _ = _mla_in_specs(m, False)
    hw = N_HEADS * QPAD
    blocks = (2 * _nbytes((m, Q_LORA), F32) + _nbytes((Q_LORA, hw), BF16)
              + 2 * _nbytes((KV_LORA, N_HEADS * QK_NOPE), BF16)
              + _nbytes((m, N_HEADS * KV_LORA), BF16) + 4 * _nbytes((m, KV_LORA), F32))
    temps = 2 * _nbytes((m, hw), F32) + 2 * _nbytes((m, N_HEADS * QK_NOPE), F32)
    return pl.pallas_call(
        _mla_sample_kernel,
        grid=(1,),
        in_specs=specs + [
            pl.BlockSpec((Q_LORA, hw), lambda i: (0, 0)),
            pl.BlockSpec((KV_LORA, N_HEADS * QK_NOPE), lambda i: (0, 0)),
            pl.BlockSpec((N_HEADS, QK_NOPE, KV_LORA), lambda i: (0, 0, 0)),
            pl.BlockSpec((1, QPAD), lambda i: (0, 0)),
            pl.BlockSpec((1, QK_NOPE), lambda i: (0, 0)),
            pl.BlockSpec((1, LANES), lambda i: (0, 0)),
            pl.BlockSpec((m, LANES), lambda i: (0, 0)),
            pl.BlockSpec((m, LANES), lambda i: (0, 0))],
        out_specs=[pl.BlockSpec((m, N_HEADS * KV_LORA), lambda i: (0, 0)),
                   pl.BlockSpec((m, N_HEADS * QK_ROPE), lambda i: (0, 0)),
                   pl.BlockSpec((m, KV_LORA), lambda i: (0, 0)),
                   pl.BlockSpec((m, QK_ROPE), lambda i: (0, 0)),
                   pl.BlockSpec((m, N_HEADS), lambda i: (0, 0))],
        out_shape=[jax.ShapeDtypeStruct((m, N_HEADS * KV_LORA), BF16),
                   jax.ShapeDtypeStruct((m, N_HEADS * QK_ROPE), BF16),
                   jax.ShapeDtypeStruct((m, KV_LORA), F32),
                   jax.ShapeDtypeStruct((m, QK_ROPE), F32),
                   jax.ShapeDtypeStruct((m, N_HEADS), F32)],
        compiler_params=pltpu.CompilerParams(
            vmem_limit_bytes=_vmem_limit(blocks, 0, temps)),
        name="mla_sample_proj",
    )(z, z, z, lw["q_norm"], lw["kv_norm"], lw["w_uq"], lw["w_uk"], lw["w_ukt"],
      lw["q_gain"], lw["k_gain_nope"], lw["k_gain_rope"], cos_t, sin_t)


def _flash_kernel(q_ref, k_ref, v_ref, o_ref, *, blk):
    qi = pl.program_id(2)
    q = q_ref[0]
    c = ATTN_SCALE * LOG2_E

    def step(j, carry, masked):
        m, l, acc = carry
        off = pl.multiple_of(j * blk, blk)
        kb = k_ref[0, pl.ds(off, blk), :]
        vb = v_ref[0, pl.ds(off, blk), :]
        s = lax.dot_general(q, kb, (((1,), (1,)), ((), ())), preferred_element_type=F32)
        if masked:
            row = lax.broadcasted_iota(jnp.int32, (blk, blk), 0)
            col = lax.broadcasted_iota(jnp.int32, (blk, blk), 1)
            s = jnp.where(col <= row, s, -jnp.inf)
        m_new = jnp.maximum(m, jnp.max(s, axis=-1, keepdims=True))
        alpha = jnp.exp2((m - m_new) * c)
        p = jnp.exp2((s - m_new) * c)
        l = alpha * l + jnp.sum(p, axis=-1, keepdims=True)
        acc = alpha * acc + jnp.dot(p.astype(BF16), vb, preferred_element_type=F32)
        return m_new, l, acc

    m0 = jnp.full((blk, 1), -jnp.inf, F32)
    l0 = jnp.zeros((blk, 1), F32)
    a0 = jnp.zeros((blk, V_DIM), F32)
    carry = lax.fori_loop(0, qi, functools.partial(step, masked=False), (m0, l0, a0))
    _, l, acc = step(qi, carry, True)
    o_ref[0] = (acc / l).astype(BF16)


def _flash(q, k, v, blk):
    batch, seq, _ = q.shape
    blocks = (_nbytes((blk, QPAD), BF16) + _nbytes((seq, QPAD), BF16) + _nbytes((seq, V_DIM), BF16)
              + _nbytes((blk, V_DIM), BF16))
    temps = 6 * _nbytes((blk, blk), F32)
    return pl.pallas_call(
        functools.partial(_flash_kernel, blk=blk),
        grid=(batch, N_HEADS, seq // blk),
        in_specs=[pl.BlockSpec((1, blk, QPAD), lambda b, h, i: (b, i, h)),
                  pl.BlockSpec((1, seq, QPAD), lambda b, h, i: (b, 0, h)),
                  pl.BlockSpec((1, seq, V_DIM), lambda b, h, i: (b, 0, h))],
        out_specs=pl.BlockSpec((1, blk, V_DIM), lambda b, h, i: (b, i, h)),
        out_shape=jax.ShapeDtypeStruct((batch, seq, N_HEADS * V_DIM), BF16),
        compiler_params=pltpu.CompilerParams(
            dimension_semantics=("parallel", "parallel", "arbitrary"),
            vmem_limit_bytes=_vmem_limit(blocks, 0, temps)),
        name="prompt_attention",
    )(q, k, v)


def _group_allreduce(x, op):
    for shift in (N_HEADS, 2 * N_HEADS, 4 * N_HEADS):
        x = op(x, pltpu.roll(x, shift, 1))
    return x


def _lane_to_row(x_lane, n_rows):
    sub = lax.broadcasted_iota(jnp.int32, (n_rows, LANES), 0)
    lane = lax.broadcasted_iota(jnp.int32, (n_rows, LANES), 1)
    return jnp.sum(jnp.where(lane == sub, x_lane, 0.0), axis=1, keepdims=True)


def _decode_kernel(pt_ref, qtc_ref, qtr_ref, c_ref, kr_ref, rl_ref, cc_hbm, ckr_hbm, cr_hbm,
                   o_ref, cbuf, krbuf, rbuf, lb, sem, m_s, l_s, acc_s,
                   *, layer, n_pages, nch, nsteps, jr):
    i = pl.program_id(0)
    slot = i % 2
    ch = i % nch

    def copies(step, slot_):
        base = (step // nch) * n_pages + (step % nch) * DEC_PAGES
        out = []
        for p in range(DEC_PAGES):
            pg = pt_ref[base + p]
            rows = pl.ds(p * PAGE_SIZE, PAGE_SIZE)
            out.append(pltpu.make_async_copy(cc_hbm.at[layer, pg], cbuf.at[slot_, rows, :], sem.at[slot_, 0]))
            out.append(pltpu.make_async_copy(ckr_hbm.at[layer, pg], krbuf.at[slot_, rows, :], sem.at[slot_, 1]))
            out.append(pltpu.make_async_copy(cr_hbm.at[layer, pg], rbuf.at[slot_, rows, :], sem.at[slot_, 2]))
        return out

    @pl.when(i == 0)
    def _():
        for cp in copies(0, 0):
            cp.start()

    @pl.when(i + 1 < nsteps)
    def _():
        for cp in copies(i + 1, 1 - slot):
            cp.start()

    qtc = qtc_ref[0]
    qtr = qtr_ref[0]
    group = lax.broadcasted_iota(jnp.int32, (1, LANES), 1) // N_HEADS

    @pl.when(ch == 0)
    def _():
        cn = c_ref[0].astype(BF16)
        krn = kr_ref[0].astype(BF16)
        s0 = (jnp.dot(jnp.broadcast_to(cn, (N_HEADS, KV_LORA)), qtc, preferred_element_type=F32)
              + jnp.dot(jnp.broadcast_to(krn, (N_HEADS, QK_ROPE)), qtr, preferred_element_type=F32))
        m_s[...] = s0 * rl_ref[0] * ATTN_SCALE
        l_s[...] = jnp.ones_like(l_s)
        acc_s[...] = jnp.broadcast_to(cn.astype(F32), acc_s.shape)

    for cp in copies(i, slot):
        cp.wait()

    s = jnp.zeros((jr, LANES), F32)
    r_parts = []
    for g in range(KEY_GROUP):
        rows = pl.ds(g * jr, jr)
        lc = cbuf[slot, rows, :].astype(BF16)
        lk = krbuf[slot, rows, :].astype(BF16)
        lb[g] = lc
        sel = group == g
        s = s + jnp.dot(lc, jnp.where(sel, qtc, jnp.zeros_like(qtc)), preferred_element_type=F32)
        s = s + jnp.dot(lk, jnp.where(sel, qtr, jnp.zeros_like(qtr)), preferred_element_type=F32)
        r_parts.append(rbuf[slot, rows, :])
    s = s * jnp.concatenate(r_parts, axis=1) * ATTN_SCALE

    m_old = m_s[0:1, :]
    m_new = jnp.maximum(m_old, _group_allreduce(jnp.max(s, axis=0, keepdims=True), jnp.maximum))
    alpha = jnp.exp(m_old - m_new)
    p = jnp.exp(s - m_new)
    l_new = alpha * l_s[0:1, :] + _group_allreduce(jnp.sum(p, axis=0, keepdims=True), jnp.add)
    m_s[...] = jnp.broadcast_to(m_new, m_s.shape)
    l_s[...] = jnp.broadcast_to(l_new, l_s.shape)

    pt = p.T.astype(BF16)
    pv = jnp.zeros((N_HEADS, KV_LORA), F32)
    for g in range(KEY_GROUP):
        pv = pv + jnp.dot(pt[g * N_HEADS:(g + 1) * N_HEADS, :], lb[g], preferred_element_type=F32)
    acc_s[...] = _lane_to_row(alpha, N_HEADS) * acc_s[...] + pv

    @pl.when(ch == nch - 1)
    def _():
        o_ref[0] = (acc_s[...] / _lane_to_row(l_new, N_HEADS)).astype(BF16)


def _decode_attention(page_table, qt_c, qt_r, c_new, kr_new, r_lane, cache_c, cache_kr, cache_r, layer):
    nseq, n_pages = page_table.shape
    assert LANES == KEY_GROUP * N_HEADS and n_pages % DEC_PAGES == 0
    nch = n_pages // DEC_PAGES
    nsteps = nseq * nch
    keys = DEC_PAGES * PAGE_SIZE
    jr = keys // KEY_GROUP
    scratch = (2 * _nbytes((keys, KV_LORA), F32) + 4 * _nbytes((keys, LANES), F32)
               + _nbytes((keys, KV_LORA), BF16) + 2 * _nbytes((N_HEADS, LANES), F32)
               + _nbytes((N_HEADS, KV_LORA), F32))
    blocks = _nbytes((KV_LORA + QK_ROPE, LANES), BF16) + 4 * _nbytes((8, KV_LORA), F32)
    temps = _nbytes((keys, KV_LORA), BF16) + 8 * _nbytes((jr, LANES), F32)
    per_seq = lambda i, pt: (i // nch, 0, 0)
    grid_spec = pltpu.PrefetchScalarGridSpec(
        num_scalar_prefetch=1,
        grid=(nsteps,),
        in_specs=[pl.BlockSpec((1, KV_LORA, LANES), per_seq),
                  pl.BlockSpec((1, QK_ROPE, LANES), per_seq),
                  pl.BlockSpec((1, 1, KV_LORA), per_seq),
                  pl.BlockSpec((1, 1, QK_ROPE), per_seq),
                  pl.BlockSpec((1, 1, LANES), per_seq),
                  pl.BlockSpec(memory_space=pl.ANY),
                  pl.BlockSpec(memory_space=pl.ANY),
                  pl.BlockSpec(memory_space=pl.ANY)],
        out_specs=pl.BlockSpec((1, N_HEADS, KV_LORA), per_seq),
        scratch_shapes=[pltpu.VMEM((2, keys, KV_LORA), F32),
                        pltpu.VMEM((2, keys, QK_ROPE), F32),
                        pltpu.VMEM((2, keys, N_HEADS), F32),
                        pltpu.VMEM((KEY_GROUP, jr, KV_LORA), BF16),
                        pltpu.SemaphoreType.DMA((2, 3)),
                        pltpu.VMEM((N_HEADS, LANES), F32),
                        pltpu.VMEM((N_HEADS, LANES), F32),
                        pltpu.VMEM((N_HEADS, KV_LORA), F32)])
    return pl.pallas_call(
        functools.partial(_decode_kernel, layer=layer, n_pages=n_pages, nch=nch, nsteps=nsteps, jr=jr),
        grid_spec=grid_spec,
        out_shape=jax.ShapeDtypeStruct((nseq, N_HEADS, KV_LORA), BF16),
        compiler_params=pltpu.CompilerParams(
            dimension_semantics=("arbitrary",),
            vmem_limit_bytes=_vmem_limit(blocks, scratch, temps)),
        name="decode_attention",
    )(page_table.reshape(-1), qt_c, qt_r, c_new, kr_new, r_lane, cache_c, cache_kr, cache_r)


def _head_out_kernel(x_ref, w_ref, o_ref):
    o_ref[...] = jnp.dot(x_ref[...], w_ref[...], preferred_element_type=F32).astype(BF16)


def _head_out(o_lat, w_uv):
    m = o_lat.shape[0]
    return pl.pallas_call(
        _head_out_kernel,
        grid=(N_HEADS,),
        in_specs=[pl.BlockSpec((m, KV_LORA), lambda h: (0, h)),
                  pl.BlockSpec((KV_LORA, V_DIM), lambda h: (0, h))],
        out_specs=pl.BlockSpec((m, V_DIM), lambda h: (0, h)),
        out_shape=jax.ShapeDtypeStruct((m, N_HEADS * V_DIM), BF16),
        compiler_params=pltpu.CompilerParams(dimension_semantics=("parallel",)),
        name="decode_head_out",
    )(o_lat, w_uv)


def _merge_kernel(ya_ref, yb_ref, yc_ref, g0_ref, g1_ref, g2_ref, wa_ref, wb_ref, wc_ref, o_ref):
    a = jnp.dot(ya_ref[...], wa_ref[...], preferred_element_type=F32)
    b = jnp.dot(yb_ref[...], wb_ref[...], preferred_element_type=F32)
    c = jnp.dot(yc_ref[...], wc_ref[...], preferred_element_type=F32)
    merged = (jax.nn.sigmoid(g0_ref[...]) * a + jax.nn.sigmoid(g1_ref[...]) * b
              + jax.nn.sigmoid(g2_ref[...]) * c)
    o_ref[...] = merged.astype(BF16)


def _merge(ya, yb, yc, zg, w_a, w_b, w_c, layer, bm):
    m = ya.shape[0]
    bn = MERGE_BN
    gate = lambda j: pl.BlockSpec((bm, bn), lambda n, i: (i, j * (D_MODEL // bn) + n))
    blocks = (2 * _nbytes((bm, WIDTH_A), BF16) + _nbytes((bm, D_MODEL), BF16)
              + 3 * _nbytes((bm, bn), F32) + 2 * _nbytes((WIDTH_A, bn), BF16)
              + _nbytes((D_MODEL, bn), BF16) + _nbytes((bm, bn), BF16))
    temps = 5 * _nbytes((bm, bn), F32)
    return pl.pallas_call(
        _merge_kernel,
        grid=(D_MODEL // bn, m // bm),
        in_specs=[pl.BlockSpec((bm, WIDTH_A), lambda n, i: (i, 0)),
                  pl.BlockSpec((bm, WIDTH_B), lambda n, i: (i, 0)),
                  pl.BlockSpec((bm, N_HEADS * V_DIM), lambda n, i: (i, 0)),
                  gate(0), gate(1), gate(2),
                  pl.BlockSpec((None, WIDTH_A, bn), lambda n, i: (layer, 0, n)),
                  pl.BlockSpec((None, WIDTH_B, bn), lambda n, i: (layer, 0, n)),
                  pl.BlockSpec((None, N_HEADS * V_DIM, bn), lambda n, i: (layer, 0, n))],
        out_specs=pl.BlockSpec((bm, bn), lambda n, i: (i, n)),
        out_shape=jax.ShapeDtypeStruct((m, D_MODEL), BF16),
        compiler_params=pltpu.CompilerParams(
            dimension_semantics=("parallel", "parallel"),
            vmem_limit_bytes=_vmem_limit(blocks, 0, temps)),
        name="merge",
    )(ya, yb, yc, zg, zg, zg, w_a, w_b, w_c)


def _out_proj_kernel(x_ref, m_ref, w_ref, o_ref):
    o_ref[...] = x_ref[...] + jnp.dot(m_ref[...], w_ref[...], preferred_element_type=F32)


def _out_proj(x, merged, w, layer, bm):
    m = x.shape[0]
    blocks = (2 * _nbytes((bm, D_MODEL), F32) + _nbytes((bm, D_MODEL), BF16)
              + _nbytes((D_MODEL, D_MODEL), BF16))
    return pl.pallas_call(
        _out_proj_kernel,
        grid=(m // bm,),
        in_specs=[pl.BlockSpec((bm, D_MODEL), lambda i: (i, 0)),
                  pl.BlockSpec((bm, D_MODEL), lambda i: (i, 0)),
                  pl.BlockSpec((None, D_MODEL, D_MODEL), lambda i: (layer, 0, 0))],
        out_specs=pl.BlockSpec((bm, D_MODEL), lambda i: (i, 0)),
        out_shape=jax.ShapeDtypeStruct((m, D_MODEL), F32),
        compiler_params=pltpu.CompilerParams(
            dimension_semantics=("parallel",),
            vmem_limit_bytes=_vmem_limit(blocks, 0, _nbytes((bm, D_MODEL), F32))),
        name="out_proj",
    )(x, merged, w)


def _rope_tables(pos):
    half = QK_ROPE // 2
    inv = jnp.power(ROPE_THETA, -jnp.arange(half, dtype=F32) / half)
    ang = pos.astype(F32)[:, None] * inv[None, :]
    zeros = jnp.zeros((pos.shape[0], LANES - QK_ROPE), F32)
    cos_t = jnp.concatenate([jnp.cos(ang), jnp.cos(ang), zeros], axis=-1)
    sin_t = jnp.concatenate([jnp.sin(ang), jnp.sin(ang), zeros], axis=-1)
    return cos_t, sin_t


def kernel(x_prompt, x_sample, cache_latent, cache_krope, cache_kscale, state_pool, page_table, ffn1_norm, ffn1_up, ffn1_down, mix_norm, w_in, gmlp_v_norm, gmlp_ws, gmlp_b, pool_w, pool_scale, mla_q_norm, mla_w_uq, mla_kv_norm, mla_w_uk, mla_w_uv, mla_q_gain, mla_k_gain, w_o_a, w_o_b, w_o_c, w_out, ffn2_norm, ffn2_up, ffn2_down):
    batch, seq, _ = x_prompt.shape
    nseq = x_sample.shape[0]
    depth = w_in.shape[0]
    n_pages = page_table.shape[1]
    past = n_pages * cache_latent.shape[2]
    off_g = Z_KR + QK_ROPE

    cos_p, sin_p = _rope_tables(jnp.arange(seq))
    cos_s, sin_s = _rope_tables(jnp.full((nseq,), past))
    tril = jnp.tril(jnp.ones((CHUNK, CHUNK), dtype=bool))
    dga = WIDTH_A // GROUPS_A

    up1, down1 = ffn1_up.astype(BF16), ffn1_down.astype(BF16)
    up2, down2 = ffn2_up.astype(BF16), ffn2_down.astype(BF16)
    w_mix = jnp.pad(w_in[:, :, :off_g], ((0, 0), (0, 0), (0, LANES - QK_ROPE))).astype(BF16)
    w_gates = w_in[:, :, off_g:].astype(BF16)
    woa, wob, woc, wout = w_o_a.astype(BF16), w_o_b.astype(BF16), w_o_c.astype(BF16), w_out.astype(BF16)

    yp = x_prompt.reshape(batch * seq, D_MODEL)
    ys = x_sample.reshape(nseq, D_MODEL)
    outs = {k: [] for k in ("lat_p", "kr_p", "r_p", "pool_p", "lat_s", "kr_s", "r_s", "pool_s", "v_s")}
    row = lambda a: a.reshape(1, -1)

    for l in range(depth):
        wq = jnp.pad(mla_w_uq[l], ((0, 0), (0, 0), (0, QPAD - QK_DIM))).reshape(Q_LORA, N_HEADS * QPAD).astype(BF16)
        lw = {
            "q_norm": row(mla_q_norm[l]), "kv_norm": row(mla_kv_norm[l]),
            "w_uq": wq,
            "w_uk": mla_w_uk[l].reshape(KV_LORA, N_HEADS * QK_NOPE).astype(BF16),
            "w_ukt": jnp.transpose(mla_w_uk[l], (1, 2, 0)).astype(BF16),
            "w_uv": mla_w_uv[l].reshape(KV_LORA, N_HEADS * V_DIM).astype(BF16),
            "q_gain": row(jnp.pad(mla_q_gain[l], (0, QPAD - QK_DIM))),
            "k_gain_nope": row(mla_k_gain[l][:QK_NOPE]),
            "k_gain_rope": row(jnp.pad(mla_k_gain[l][QK_NOPE:], (0, LANES - QK_ROPE))),
        }
        ws_tril = jnp.where(tril, gmlp_ws[l], 0).astype(BF16)
        pw = pool_w[l].astype(BF16)

        yp = _ffn(yp, row(ffn1_norm[l]), up1, down1, l, ROW_TILE)
        z = _in_proj(yp, row(mix_norm[l]), w_mix, l, IN_BM, IN_BN_MIX)
        zg = _in_proj(yp, row(mix_norm[l]), w_gates, l, IN_BM, IN_BN_GATES)
        ya = _gmlp_prompt(z, row(gmlp_v_norm[l]), ws_tril, gmlp_b[l].T, ROW_TILE)
        yb, pool_new = _pool_prompt(z, pw, row(pool_scale[l]), batch, seq, ROW_TILE)
        q, k, v, c, kr, r = _mla_prompt(z, lw, cos_p, sin_p, seq, MLA_BT)
        yc = _flash(q.reshape(batch, seq, -1), k.reshape(batch, seq, -1), v.reshape(batch, seq, -1), ATT_BLK)
        merged = _merge(ya, yb, yc.reshape(batch * seq, -1), zg, woa, wob, woc, l, ROW_TILE)
        yp = _out_proj(yp, merged, wout, l, ROW_TILE)
        yp = _ffn(yp, row(ffn2_norm[l]), up2, down2, l, ROW_TILE)
        outs["lat_p"].append(c.reshape(batch, seq, KV_LORA))
        outs["kr_p"].append(kr.reshape(batch, seq, QK_ROPE))
        outs["r_p"].append(r.reshape(batch, seq, N_HEADS))
        outs["pool_p"].append(pool_new)

        ys = _ffn(ys, row(ffn1_norm[l]), up1, down1, l, nseq)
        zs = _in_proj(ys, row(mix_norm[l]), w_mix, l, nseq, IN_BN_MIX)
        zgs = _in_proj(ys, row(mix_norm[l]), w_gates, l, nseq, IN_BN_GATES)
        w0 = jnp.repeat(gmlp_ws[l][:, 0, 0].astype(BF16).astype(F32), dga)
        b0 = jnp.repeat(gmlp_b[l][:, 0], dga)
        ya_s, v_s = _gmlp_sample(zs, row(gmlp_v_norm[l]), row(w0), row(b0))
        yb_s = _pool_sample(zs, jnp.swapaxes(state_pool[l], 0, 1), pw, row(pool_scale[l]), past)
        ql, qr, c_s, kr_s, r_s = _mla_sample(zs, lw, cos_s, sin_s)
        qt_c = jnp.tile(jnp.swapaxes(ql.reshape(nseq, N_HEADS, KV_LORA), 1, 2), (1, 1, KEY_GROUP))
        qt_r = jnp.tile(jnp.swapaxes(qr.reshape(nseq, N_HEADS, QK_ROPE), 1, 2), (1, 1, KEY_GROUP))
        r_lane = jnp.tile(r_s, (1, KEY_GROUP)).reshape(nseq, 1, LANES)
        o_lat = _decode_attention(
            page_table, qt_c, qt_r, c_s.reshape(nseq, 1, KV_LORA), kr_s.reshape(nseq, 1, QK_ROPE), r_lane,
            cache_latent, cache_krope, cache_kscale, l)
        yc_s = _head_out(o_lat.reshape(nseq, N_HEADS * KV_LORA), lw["w_uv"])
        merged_s = _merge(ya_s, yb_s, yc_s, zgs, woa, wob, woc, l, nseq)
        ys = _out_proj(ys, merged_s, wout, l, nseq)
        ys = _ffn(ys, row(ffn2_norm[l]), up2, down2, l, nseq)
        outs["lat_s"].append(c_s.reshape(nseq, 1, KV_LORA))
        outs["kr_s"].append(kr_s.reshape(nseq, 1, QK_ROPE))
        outs["r_s"].append(r_s.reshape(nseq, 1, N_HEADS))
        outs["pool_s"].append(jnp.concatenate(
            [state_pool[l][:, 1:], zs[:, Z_B:Z_B + WIDTH_B][:, None, :]], axis=1))
        outs["v_s"].append(v_s.reshape(nseq, 1, WIDTH_A))

    st = lambda key: jnp.stack(outs[key])
    return (yp.reshape(batch, seq, D_MODEL), ys.reshape(nseq, 1, D_MODEL),
            st("lat_p"), st("kr_p"), st("r_p"), st("pool_p"),
            st("lat_s"), st("kr_s"), st("r_s"), st("pool_s"), st("v_s"))
```

```python
import functools

import jax
import jax.numpy as jnp
from jax import lax
from jax.experimental import pallas as pl
from jax.experimental.pallas import tpu as pltpu

F32 = jnp.float32
BF16 = jnp.bfloat16

D_MODEL = 2048
D_FF = 5632
WIDTH_A = 1024
GROUPS_A = 4
CHUNK = 128
WIDTH_B = 1024
POOL_WINDOWS = (2, 4, 8, 16)
POOL_BUF = 15
N_HEADS = 16
QK_NOPE = 128
QK_ROPE = 64
QK_DIM = QK_NOPE + QK_ROPE
V_DIM = 128
Q_LORA = 512
KV_LORA = 512
ROPE_THETA = 10000.0
ATTN_SCALE = QK_DIM ** -0.5
LOG2_E = 1.4426950408889634
EPS = 1e-6
PAGE_SIZE = 128

LANES = 128
VMEM_BYTES = 64 * 1024 * 1024

Z_A = 0
Z_B = 2 * WIDTH_A
Z_Q = Z_B + WIDTH_B
Z_KV = Z_Q + Q_LORA
Z_KR = Z_KV + KV_LORA
Z_MIX = Z_KR + LANES
Z_GATES = 3 * D_MODEL
QPAD = 2 * LANES

ROW_TILE = 512
FFN_TF = 512
IN_BM = 1024
IN_BN_MIX = 1408
IN_BN_GATES = 1024
MLA_BT = 256
MERGE_BN = 1024
ATT_BLK = 512
DEC_PAGES = 32


def _vmem_limit(block_bytes, scratch_bytes, temp_bytes):
    need = 2 * block_bytes + scratch_bytes + temp_bytes
    return int(min(need, VMEM_BYTES - 4 * 1024 * 1024))


def _nbytes(shape, dtype):
    n = 1
    for s in shape:
        n *= s
    return n * jnp.dtype(dtype).itemsize


def _rms(x, g):
    return x * lax.rsqrt(jnp.mean(x * x, axis=-1, keepdims=True) + EPS) * g


def _ffn_kernel(x_ref, g_ref, wg_ref, wu_ref, wd_ref, o_ref, h_ref, acc_ref, *, nf):
    f = pl.program_id(1)

    @pl.when(f == 0)
    def _():
        h_ref[...] = _rms(x_ref[...], g_ref[...]).astype(BF16)
        acc_ref[...] = jnp.zeros_like(acc_ref)

    h = h_ref[...]
    g = jnp.dot(h, wg_ref[...], preferred_element_type=F32)
    u = jnp.dot(h, wu_ref[...], preferred_element_type=F32)
    a = (g * jax.nn.sigmoid(g) * u).astype(BF16)
    acc_ref[...] += jnp.dot(a, wd_ref[...], preferred_element_type=F32)

    @pl.when(f == nf - 1)
    def _():
        o_ref[...] = x_ref[...] + 0.5 * acc_ref[...]


def _ffn(x, norm_g, w_up, w_down, layer, bm):
    m = x.shape[0]
    nf = D_FF // FFN_TF
    blocks = (2 * _nbytes((bm, D_MODEL), F32) + 2 * _nbytes((D_MODEL, FFN_TF), BF16)
              + _nbytes((FFN_TF, D_MODEL), BF16))
    scratch = _nbytes((bm, D_MODEL), BF16) + _nbytes((bm, D_MODEL), F32)
    temps = 4 * _nbytes((bm, FFN_TF), F32) + _nbytes((bm, D_MODEL), F32)
    return pl.pallas_call(
        functools.partial(_ffn_kernel, nf=nf),
        grid=(m // bm, nf),
        in_specs=[pl.BlockSpec((bm, D_MODEL), lambda i, f: (i, 0)),
                  pl.BlockSpec((1, D_MODEL), lambda i, f: (0, 0)),
                  pl.BlockSpec((None, D_MODEL, FFN_TF), lambda i, f: (layer, 0, f)),
                  pl.BlockSpec((None, D_MODEL, FFN_TF), lambda i, f: (layer, 0, f + nf)),
                  pl.BlockSpec((None, FFN_TF, D_MODEL), lambda i, f: (layer, f, 0))],
        out_specs=pl.BlockSpec((bm, D_MODEL), lambda i, f: (i, 0)),
        out_shape=jax.ShapeDtypeStruct((m, D_MODEL), F32),
        scratch_shapes=[pltpu.VMEM((bm, D_MODEL), BF16), pltpu.VMEM((bm, D_MODEL), F32)],
        compiler_params=pltpu.CompilerParams(
            dimension_semantics=("parallel", "arbitrary"),
            vmem_limit_bytes=_vmem_limit(blocks, scratch, temps)),
        name="ffn",
    )(x, norm_g, w_up, w_up, w_down)


def _in_proj_kernel(x_ref, g_ref, w_ref, o_ref, h_ref):
    @pl.when(pl.program_id(1) == 0)
    def _():
        h_ref[...] = _rms(x_ref[...], g_ref[...]).astype(BF16)

    o_ref[...] = jnp.dot(h_ref[...], w_ref[...], preferred_element_type=F32)


def _in_proj(x, norm_g, w, layer, bm, bn):
    m = x.shape[0]
    n_out = w.shape[2]
    blocks = (_nbytes((bm, D_MODEL), F32) + _nbytes((D_MODEL, bn), BF16)
              + _nbytes((bm, bn), F32))
    scratch = _nbytes((bm, D_MODEL), BF16)
    temps = _nbytes((bm, D_MODEL), F32) + _nbytes((bm, bn), F32)
    return pl.pallas_call(
        _in_proj_kernel,
        grid=(m // bm, n_out // bn),
        in_specs=[pl.BlockSpec((bm, D_MODEL), lambda i, n: (i, 0)),
                  pl.BlockSpec((1, D_MODEL), lambda i, n: (0, 0)),
                  pl.BlockSpec((None, D_MODEL, bn), lambda i, n: (layer, 0, n))],
        out_specs=pl.BlockSpec((bm, bn), lambda i, n: (i, n)),
        out_shape=jax.ShapeDtypeStruct((m, n_out), F32),
        scratch_shapes=[pltpu.VMEM((bm, D_MODEL), BF16)],
        compiler_params=pltpu.CompilerParams(
            dimension_semantics=("parallel", "arbitrary"),
            vmem_limit_bytes=_vmem_limit(blocks, scratch, temps)),
        name="in_proj",
    )(x, norm_g, w)


def _gmlp_prompt_kernel(z_ref, vg_ref, ws_ref, b_ref, ya_ref, *, bt):
    z = jax.nn.gelu(z_ref[...])
    u = z[:, :WIDTH_A]
    v = _rms(z[:, WIDTH_A:], vg_ref[...]).astype(BF16)
    dg = WIDTH_A // GROUPS_A
    for c in range(bt // CHUNK):
        rows = slice(c * CHUNK, (c + 1) * CHUNK)
        for g in range(GROUPS_A):
            cols = slice(g * dg, (g + 1) * dg)
            s = jnp.dot(ws_ref[g], v[rows, cols], preferred_element_type=F32) + b_ref[:, g:g + 1]
            ya_ref[rows, cols] = (u[rows, cols] * s).astype(BF16)


def _gmlp_prompt(z, v_gain, ws_tril, b_t, bt):
    m = z.shape[0]
    blocks = _nbytes((bt, 2 * WIDTH_A), F32) + _nbytes((bt, WIDTH_A), BF16) + _nbytes((GROUPS_A, CHUNK, CHUNK), BF16)
    temps = 3 * _nbytes((bt, 2 * WIDTH_A), F32)
    return pl.pallas_call(
        functools.partial(_gmlp_prompt_kernel, bt=bt),
        grid=(m // bt,),
        in_specs=[pl.BlockSpec((bt, 2 * WIDTH_A), lambda i: (i, Z_A // (2 * WIDTH_A))),
                  pl.BlockSpec((1, WIDTH_A), lambda i: (0, 0)),
                  pl.BlockSpec((GROUPS_A, CHUNK, CHUNK), lambda i: (0, 0, 0)),
                  pl.BlockSpec((CHUNK, GROUPS_A), lambda i: (0, 0))],
        out_specs=pl.BlockSpec((bt, WIDTH_A), lambda i: (i, 0)),
        out_shape=jax.ShapeDtypeStruct((m, WIDTH_A), BF16),
        compiler_params=pltpu.CompilerParams(
            dimension_semantics=("parallel",),
            vmem_limit_bytes=_vmem_limit(blocks, 0, temps)),
        name="gmlp_prompt",
    )(z, v_gain, ws_tril, b_t)


def _gmlp_sample_kernel(z_ref, vg_ref, w0_ref, b0_ref, ya_ref, v_ref):
    z = jax.nn.gelu(z_ref[...])
    u = z[:, :WIDTH_A]
    v = _rms(z[:, WIDTH_A:], vg_ref[...])
    v_ref[...] = v
    s = v.astype(BF16).astype(F32) * w0_ref[...] + b0_ref[...]
    ya_ref[...] = (u * s).astype(BF16)


def _gmlp_sample(z, v_gain, w0, b0):
    m = z.shape[0]
    return pl.pallas_call(
        _gmlp_sample_kernel,
        grid=(1,),
        in_specs=[pl.BlockSpec((m, 2 * WIDTH_A), lambda i: (0, Z_A // (2 * WIDTH_A))),
                  pl.BlockSpec((1, WIDTH_A), lambda i: (0, 0)),
                  pl.BlockSpec((1, WIDTH_A), lambda i: (0, 0)),
                  pl.BlockSpec((1, WIDTH_A), lambda i: (0, 0))],
        out_specs=[pl.BlockSpec((m, WIDTH_A), lambda i: (0, 0)),
                   pl.BlockSpec((m, WIDTH_A), lambda i: (0, 0))],
        out_shape=[jax.ShapeDtypeStruct((m, WIDTH_A), BF16),
                   jax.ShapeDtypeStruct((m, WIDTH_A), F32)],
        name="gmlp_sample",
    )(z, v_gain, w0, b0)


HALO = 16


def _pool_prompt_kernel(p_ref, w_ref, sc_ref, yb_ref, new_ref, xp_ref, *, bt, nt):
    t = pl.program_id(1)

    @pl.when(t == 0)
    def _():
        xp_ref[0:HALO, :] = jnp.zeros((HALO, WIDTH_B), F32)

    @pl.when(t > 0)
    def _():
        xp_ref[0:HALO, :] = xp_ref[bt:bt + HALO, :]

    p = p_ref[...]
    xp_ref[HALO:HALO + bt, :] = p
    pos = t * bt + lax.broadcasted_iota(jnp.int32, (bt, 1), 0)
    dg = WIDTH_B // len(POOL_WINDOWS)
    for gi, w in enumerate(POOL_WINDOWS):
        cols = slice(gi * dg, (gi + 1) * dg)
        win = p[:, cols]
        for j in range(1, w):
            win = win + xp_ref[HALO - j:HALO - j + bt, cols]
        cnt = jnp.minimum(w, pos + 1).astype(F32)
        d = (win / cnt - p[:, cols]).astype(BF16)
        y = jnp.dot(d, w_ref[gi], preferred_element_type=F32) * sc_ref[:, cols]
        yb_ref[:, cols] = y.astype(BF16)

    @pl.when(t == nt - 1)
    def _():
        new_ref[0] = xp_ref[HALO + bt - POOL_BUF:HALO + bt, :]


def _pool_prompt(z, pool_w, scale, batch, seq, bt):
    nt = seq // bt
    dg = WIDTH_B // len(POOL_WINDOWS)
    blocks = (_nbytes((bt, WIDTH_B), F32) + _nbytes((len(POOL_WINDOWS), dg, dg), BF16)
              + _nbytes((bt, WIDTH_B), BF16) + _nbytes((16, WIDTH_B), F32))
    scratch = _nbytes((bt + HALO, WIDTH_B), F32)
    temps = 4 * _nbytes((bt, WIDTH_B), F32)
    return pl.pallas_call(
        functools.partial(_pool_prompt_kernel, bt=bt, nt=nt),
        grid=(batch, nt),
        in_specs=[pl.BlockSpec((bt, WIDTH_B), lambda b, t: (b * nt + t, Z_B // WIDTH_B)),
                  pl.BlockSpec((len(POOL_WINDOWS), dg, dg), lambda b, t: (0, 0, 0)),
                  pl.BlockSpec((1, WIDTH_B), lambda b, t: (0, 0))],
        out_specs=[pl.BlockSpec((bt, WIDTH_B), lambda b, t: (b * nt + t, 0)),
                   pl.BlockSpec((1, POOL_BUF, WIDTH_B), lambda b, t: (b, 0, 0))],
        out_shape=[jax.ShapeDtypeStruct((batch * seq, WIDTH_B), BF16),
                   jax.ShapeDtypeStruct((batch, POOL_BUF, WIDTH_B), F32)],
        scratch_shapes=[pltpu.VMEM((bt + HALO, WIDTH_B), F32)],
        compiler_params=pltpu.CompilerParams(
            dimension_semantics=("parallel", "arbitrary"),
            vmem_limit_bytes=_vmem_limit(blocks, scratch, temps)),
        name="pool_prompt",
    )(z, pool_w, scale)


def _pool_sample_kernel(p_ref, prev_ref, w_ref, sc_ref, yb_ref, *, pos0):
    p = p_ref[...]
    dg = WIDTH_B // len(POOL_WINDOWS)
    for gi, w in enumerate(POOL_WINDOWS):
        cols = slice(gi * dg, (gi + 1) * dg)
        win = p[:, cols]
        for j in range(1, w):
            win = win + prev_ref[POOL_BUF - j, :, cols]
        cnt = float(min(w, pos0 + 1))
        d = (win / cnt - p[:, cols]).astype(BF16)
        y = jnp.dot(d, w_ref[gi], preferred_element_type=F32) * sc_ref[:, cols]
        yb_ref[:, cols] = y.astype(BF16)


def _pool_sample(z, prev_t, pool_w, scale, pos0):
    m = z.shape[0]
    dg = WIDTH_B // len(POOL_WINDOWS)
    blocks = (_nbytes((m, WIDTH_B), F32) + _nbytes((POOL_BUF, m, WIDTH_B), F32)
              + _nbytes((len(POOL_WINDOWS), dg, dg), BF16) + _nbytes((m, WIDTH_B), BF16))
    return pl.pallas_call(
        functools.partial(_pool_sample_kernel, pos0=pos0),
        grid=(1,),
        in_specs=[pl.BlockSpec((m, WIDTH_B), lambda i: (0, Z_B // WIDTH_B)),
                  pl.BlockSpec((POOL_BUF, m, WIDTH_B), lambda i: (0, 0, 0)),
                  pl.BlockSpec((len(POOL_WINDOWS), dg, dg), lambda i: (0, 0, 0)),
                  pl.BlockSpec((1, WIDTH_B), lambda i: (0, 0))],
        out_specs=pl.BlockSpec((m, WIDTH_B), lambda i: (0, 0)),
        out_shape=jax.ShapeDtypeStruct((m, WIDTH_B), BF16),
        compiler_params=pltpu.CompilerParams(
            vmem_limit_bytes=_vmem_limit(blocks, 0, 4 * _nbytes((m, WIDTH_B), F32))),
        name="pool_sample",
    )(z, prev_t, pool_w, scale)


def _rope(r, cos_t, sin_t, lane):
    rot = jnp.where(lane < QK_ROPE // 2, -pltpu.roll(r, LANES - QK_ROPE // 2, 1),
                    pltpu.roll(r, QK_ROPE // 2, 1))
    return r * cos_t + rot * sin_t


def _mla_common(hq_ref, hkv_ref, hkr_ref, qn_ref, kvn_ref, kgr_ref, cos_ref, sin_ref,
                c_ref, kr_ref):
    cos_t = cos_ref[...]
    sin_t = sin_ref[...]
    lane = lax.broadcasted_iota(jnp.int32, (1, LANES), 1)
    qn = _rms(hq_ref[...], qn_ref[...]).astype(BF16)
    c = _rms(hkv_ref[...], kvn_ref[...])
    c_ref[...] = c
    hkr = hkr_ref[...]
    krsq = jnp.sum(hkr * hkr, axis=-1, keepdims=True)
    kr = _rope(hkr * kgr_ref[...], cos_t, sin_t, lane)
    kr_ref[...] = kr[:, :QK_ROPE]
    return qn, c.astype(BF16), krsq, kr, cos_t, sin_t, lane


def _head_query(qall, h, qg, cos_t, sin_t, lane):
    qh = qall[:, h * QPAD:(h + 1) * QPAD]
    y = qh * lax.rsqrt(jnp.sum(qh * qh, axis=-1, keepdims=True) / QK_DIM + EPS) * qg
    return y[:, :QK_NOPE], _rope(y[:, QK_NOPE:], cos_t, sin_t, lane)


def _head_rscale(kall, h, krsq):
    kn = kall[:, h * QK_NOPE:(h + 1) * QK_NOPE]
    ms = (jnp.sum(kn * kn, axis=-1, keepdims=True) + krsq) / QK_DIM
    return kn, lax.rsqrt(ms + EPS)


def _mla_prompt_kernel(hq_ref, hkv_ref, hkr_ref, qn_ref, kvn_ref, wq_ref, wuk_ref, wuv_ref,
                       qg_ref, kgn_ref, kgr_ref, cos_ref, sin_ref,
                       q_ref, k_ref, v_ref, c_ref, kr_ref, r_ref):
    qn, cb, krsq, kr, cos_t, sin_t, lane = _mla_common(
        hq_ref, hkv_ref, hkr_ref, qn_ref, kvn_ref, kgr_ref, cos_ref, sin_ref, c_ref, kr_ref)
    qall = jnp.dot(qn, wq_ref[...], preferred_element_type=F32)
    kall = jnp.dot(cb, wuk_ref[...], preferred_element_type=F32)
    v_ref[...] = jnp.dot(cb, wuv_ref[...], preferred_element_type=F32).astype(BF16)
    qg = qg_ref[...]
    kgn = kgn_ref[...]
    racc = jnp.zeros((qn.shape[0], LANES), F32)
    for h in range(N_HEADS):
        q_nope, q_rope = _head_query(qall, h, qg, cos_t, sin_t, lane)
        q_ref[:, h * QPAD:h * QPAD + QK_NOPE] = q_nope.astype(BF16)
        q_ref[:, h * QPAD + QK_NOPE:(h + 1) * QPAD] = q_rope.astype(BF16)
        kn, r = _head_rscale(kall, h, krsq)
        k_ref[:, h * QPAD:h * QPAD + QK_NOPE] = (kn * kgn * r).astype(BF16)
        k_ref[:, h * QPAD + QK_NOPE:(h + 1) * QPAD] = (kr * r).astype(BF16)
        racc = jnp.where(lane == h, r, racc)
    r_ref[...] = racc[:, :N_HEADS]


def _mla_sample_kernel(hq_ref, hkv_ref, hkr_ref, qn_ref, kvn_ref, wq_ref, wuk_ref, wukt_ref,
                       qg_ref, kgn_ref, kgr_ref, cos_ref, sin_ref,
                       ql_ref, qr_ref, c_ref, kr_ref, r_ref):
    qn, cb, krsq, kr, cos_t, sin_t, lane = _mla_common(
        hq_ref, hkv_ref, hkr_ref, qn_ref, kvn_ref, kgr_ref, cos_ref, sin_ref, c_ref, kr_ref)
    qall = jnp.dot(qn, wq_ref[...], preferred_element_type=F32)
    kall = jnp.dot(cb, wuk_ref[...], preferred_element_type=F32)
    qg = qg_ref[...]
    kgn = kgn_ref[...]
    racc = jnp.zeros((qn.shape[0], LANES), F32)
    q_rope_prev = None
    for h in range(N_HEADS):
        q_nope, q_rope = _head_query(qall, h, qg, cos_t, sin_t, lane)
        ql = jnp.dot((q_nope * kgn).astype(BF16), wukt_ref[h], preferred_element_type=F32)
        ql_ref[:, h * KV_LORA:(h + 1) * KV_LORA] = ql.astype(BF16)
        if h % 2 == 0:
            q_rope_prev = q_rope
        else:
            pair = q_rope_prev + pltpu.roll(q_rope, QK_ROPE, 1)
            qr_ref[:, (h - 1) * QK_ROPE:(h + 1) * QK_ROPE] = pair.astype(BF16)
        _, r = _head_rscale(kall, h, krsq)
        racc = jnp.where(lane == h, r, racc)
    r_ref[...] = racc[:, :N_HEADS]


def _mla_in_specs(bt, cos_rows):
    tab = ((lambda i: (i, 0)) if cos_rows else (lambda i: (0, 0)))
    return [pl.BlockSpec((bt, Q_LORA), lambda i: (i, Z_Q // Q_LORA)),
            pl.BlockSpec((bt, KV_LORA), lambda i: (i, Z_KV // KV_LORA)),
            pl.BlockSpec((bt, LANES), lambda i: (i, Z_KR // LANES)),
            pl.BlockSpec((1, Q_LORA), lambda i: (0, 0)),
            pl.BlockSpec((1, KV_LORA), lambda i: (0, 0))], tab


def _mla_prompt(z, lw, cos_t, sin_t, seq, bt):
    m = z.shape[0]
    nt = seq // bt
    specs, _ = _mla_in_specs(bt, True)
    hw = N_HEADS * QPAD
    blocks = (2 * _nbytes((bt, Q_LORA), F32) + _nbytes((Q_LORA, hw), BF16)
              + 2 * _nbytes((KV_LORA, N_HEADS * QK_NOPE), BF16) + 2 * _nbytes((bt, hw), BF16)
              + _nbytes((bt, N_HEADS * V_DIM), BF16) + 4 * _nbytes((bt, KV_LORA), F32))
    temps = 2 * _nbytes((bt, hw), F32) + 2 * _nbytes((bt, N_HEADS * QK_NOPE), F32)
    return pl.pallas_call(
        _mla_prompt_kernel,
        grid=(m // bt,),
        in_specs=specs + [
            pl.BlockSpec((Q_LORA, hw), lambda i: (0, 0)),
            pl.BlockSpec((KV_LORA, N_HEADS * QK_NOPE), lambda i: (0, 0)),
            pl.BlockSpec((KV_LORA, N_HEADS * V_DIM), lambda i: (0, 0)),
            pl.BlockSpec((1, QPAD), lambda i: (0, 0)),
            pl.BlockSpec((1, QK_NOPE), lambda i: (0, 0)),
            pl.BlockSpec((1, LANES), lambda i: (0, 0)),
            pl.BlockSpec((bt, LANES), lambda i: (i % nt, 0)),
            pl.BlockSpec((bt, LANES), lambda i: (i % nt, 0))],
        out_specs=[pl.BlockSpec((bt, hw), lambda i: (i, 0)),
                   pl.BlockSpec((bt, hw), lambda i: (i, 0)),
                   pl.BlockSpec((bt, N_HEADS * V_DIM), lambda i: (i, 0)),
                   pl.BlockSpec((bt, KV_LORA), lambda i: (i, 0)),
                   pl.BlockSpec((bt, QK_ROPE), lambda i: (i, 0)),
                   pl.BlockSpec((bt, N_HEADS), lambda i: (i, 0))],
        out_shape=[jax.ShapeDtypeStruct((m, hw), BF16),
                   jax.ShapeDtypeStruct((m, hw), BF16),
                   jax.ShapeDtypeStruct((m, N_HEADS * V_DIM), BF16),
                   jax.ShapeDtypeStruct((m, KV_LORA), F32),
                   jax.ShapeDtypeStruct((m, QK_ROPE), F32),
                   jax.ShapeDtypeStruct((m, N_HEADS), F32)],
        compiler_params=pltpu.CompilerParams(
            dimension_semantics=("parallel",),
            vmem_limit_bytes=_vmem_limit(blocks, 0, temps)),
        name="mla_prompt_proj",
    )(z, z, z, lw["q_norm"], lw["kv_norm"], lw["w_uq"], lw["w_uk"], lw["w_uv"],
      lw["q_gain"], lw["k_gain_nope"], lw["k_gain_rope"], cos_t, sin_t)


def _mla_sample(z, lw, cos_t, sin_t):
    m = z.shape[0]
    specs, _ = _mla_in_specs(m, False)
    hw = N_HEADS * QPAD
    blocks = (2 * _nbytes((m, Q_LORA), F32) + _nbytes((Q_LORA, hw), BF16)
              + 2 * _nbytes((KV_LORA, N_HEADS * QK_NOPE), BF16)
              + _nbytes((m, N_HEADS * KV_LORA), BF16) + 4 * _nbytes((m, KV_LORA), F32))
    temps = 2 * _nbytes((m, hw), F32) + 2 * _nbytes((m, N_HEADS * QK_NOPE), F32)
    return pl.pallas_call(
        _mla_sample_kernel,
        grid=(1,),
        in_specs=specs + [
            pl.BlockSpec((Q_LORA, hw), lambda i: (0, 0)),
            pl.BlockSpec((KV_LORA, N_HEADS * QK_NOPE), lambda i: (0, 0)),
            pl.BlockSpec((N_HEADS, QK_NOPE, KV_LORA), lambda i: (0, 0, 0)),
            pl.BlockSpec((1, QPAD), lambda i: (0, 0)),
            pl.BlockSpec((1, QK_NOPE), lambda i: (0, 0)),
            pl.BlockSpec((1, LANES), lambda i: (0, 0)),
            pl.BlockSpec((m, LANES), lambda i: (0, 0)),
            pl.BlockSpec((m, LANES), lambda i: (0, 0))],
        out_specs=[pl.BlockSpec((m, N_HEADS * KV_LORA), lambda i: (0, 0)),
                   pl.BlockSpec((m, N_HEADS * QK_ROPE), lambda i: (0, 0)),
                   pl.BlockSpec((m, KV_LORA), lambda i: (0, 0)),
                   pl.BlockSpec((m, QK_ROPE), lambda i: (0, 0)),
                   pl.BlockSpec((m, N_HEADS), lambda i: (0, 0))],
        out_shape=[jax.ShapeDtypeStruct((m, N_HEADS * KV_LORA), BF16),
                   jax.ShapeDtypeStruct((m, N_HEADS * QK_ROPE), BF16),
                   jax.ShapeDtypeStruct((m, KV_LORA), F32),
                   jax.ShapeDtypeStruct((m, QK_ROPE), F32),
                   jax.ShapeDtypeStruct((m, N_HEADS), F32)],
        compiler_params=pltpu.CompilerParams(
            vmem_limit_bytes=_vmem_limit(blocks, 0, temps)),
        name="mla_sample_proj",
    )(z, z, z, lw["q_norm"], lw["kv_norm"], lw["w_uq"], lw["w_uk"], lw["w_ukt"],
      lw["q_gain"], lw["k_gain_nope"], lw["k_gain_rope"], cos_t, sin_t)


def _flash_kernel(q_ref, k_ref, v_ref, o_ref, *, blk):
    qi = pl.program_id(2)
    q = q_ref[0]
    c = ATTN_SCALE * LOG2_E

    def step(j, carry, masked):
        m, l, acc = carry
        off = pl.multiple_of(j * blk, blk)
        kb = k_ref[0, pl.ds(off, blk), :]
        vb = v_ref[0, pl.ds(off, blk), :]
        s = lax.dot_general(q, kb, (((1,), (1,)), ((), ())), preferred_element_type=F32)
        if masked:
            row = lax.broadcasted_iota(jnp.int32, (blk, blk), 0)
            col = lax.broadcasted_iota(jnp.int32, (blk, blk), 1)
            s = jnp.where(col <= row, s, -jnp.inf)
        m_new = jnp.maximum(m, jnp.max(s, axis=-1, keepdims=True))
        alpha = jnp.exp2((m - m_new) * c)
        p = jnp.exp2((s - m_new) * c)
        l = alpha * l + jnp.sum(p, axis=-1, keepdims=True)
        acc = alpha * acc + jnp.dot(p.astype(BF16), vb, preferred_element_type=F32)
        return m_new, l, acc

    m0 = jnp.full((blk, 1), -jnp.inf, F32)
    l0 = jnp.zeros((blk, 1), F32)
    a0 = jnp.zeros((blk, V_DIM), F32)
    carry = lax.fori_loop(0, qi, functools.partial(step, masked=False), (m0, l0, a0))
    _, l, acc = step(qi, carry, True)
    o_ref[0] = (acc / l).astype(BF16)


def _flash(q, k, v, blk):
    batch, seq, _ = q.shape
    blocks = (_nbytes((blk, QPAD), BF16) + _nbytes((seq, QPAD), BF16) + _nbytes((seq, V_DIM), BF16)
              + _nbytes((blk, V_DIM), BF16))
    temps = 6 * _nbytes((blk, blk), F32)
    return pl.pallas_call(
        functools.partial(_flash_kernel, blk=blk),
        grid=(batch, N_HEADS, seq // blk),
        in_specs=[pl.BlockSpec((1, blk, QPAD), lambda b, h, i: (b, i, h)),
                  pl.BlockSpec((1, seq, QPAD), lambda b, h, i: (b, 0, h)),
                  pl.BlockSpec((1, seq, V_DIM), lambda b, h, i: (b, 0, h))],
        out_specs=pl.BlockSpec((1, blk, V_DIM), lambda b, h, i: (b, i, h)),
        out_shape=jax.ShapeDtypeStruct((batch, seq, N_HEADS * V_DIM), BF16),
        compiler_params=pltpu.CompilerParams(
            dimension_semantics=("parallel", "parallel", "arbitrary"),
            vmem_limit_bytes=_vmem_limit(blocks, 0, temps)),
        name="prompt_attention",
    )(q, k, v)


def _decode_kernel(pt_ref, ql_ref, qr_ref, c_ref, kr_ref, r_ref, cc_hbm, ckrt_hbm, crt_hbm,
                   o_ref, cbuf, krbuf, rbuf, sem, m_s, l_s, acc_s, *, layer, n_pages, nch, nsteps):
    i = pl.program_id(0)
    slot = i % 2
    ch = i % nch

    def copies(step, slot_):
        base = (step // nch) * n_pages + (step % nch) * DEC_PAGES
        out = []
        for p in range(DEC_PAGES):
            pg = pt_ref[base + p]
            keys = pl.ds(p * PAGE_SIZE, PAGE_SIZE)
            out.append(pltpu.make_async_copy(cc_hbm.at[layer, pg], cbuf.at[slot_, keys, :], sem.at[slot_, 0]))
            out.append(pltpu.make_async_copy(ckrt_hbm.at[layer, pg], krbuf.at[slot_, :, keys], sem.at[slot_, 1]))
            out.append(pltpu.make_async_copy(crt_hbm.at[layer, pg], rbuf.at[slot_, :, keys], sem.at[slot_, 2]))
        return out

    @pl.when(i == 0)
    def _():
        for cp in copies(0, 0):
            cp.start()

    @pl.when(i + 1 < nsteps)
    def _():
        for cp in copies(i + 1, 1 - slot):
            cp.start()

    ql = ql_ref[0]
    qr = qr_ref[0]

    @pl.when(ch == 0)
    def _():
        cn = c_ref[0].astype(BF16).astype(F32)
        krn = kr_ref[0].astype(BF16).astype(F32)
        s0 = (jnp.sum(ql.astype(F32) * cn, axis=-1, keepdims=True)
              + jnp.sum(qr.astype(F32) * krn, axis=-1, keepdims=True))
        s0 = s0 * r_ref[0] * ATTN_SCALE
        m_s[...] = jnp.broadcast_to(s0, m_s.shape)
        l_s[...] = jnp.ones_like(l_s)
        acc_s[...] = jnp.broadcast_to(cn, acc_s.shape)

    for cp in copies(i, slot):
        cp.wait()

    cb = cbuf[slot].astype(BF16)
    krb = krbuf[slot].astype(BF16)
    s = (lax.dot_general(ql, cb, (((1,), (1,)), ((), ())), preferred_element_type=F32)
         + jnp.dot(qr, krb, preferred_element_type=F32))
    s = s * rbuf[slot] * ATTN_SCALE
    m_prev = m_s[:, :1]
    m_new = jnp.maximum(m_prev, jnp.max(s, axis=-1, keepdims=True))
    alpha = jnp.exp(m_prev - m_new)
    p = jnp.exp(s - m_new)
    l_s[...] = alpha * l_s[...] + jnp.sum(p, axis=-1, keepdims=True)
    acc_s[...] = alpha * acc_s[...] + jnp.dot(p.astype(BF16), cb, preferred_element_type=F32)
    m_s[...] = jnp.broadcast_to(m_new, m_s.shape)

    @pl.when(ch == nch - 1)
    def _():
        o_ref[0] = (acc_s[...] / l_s[:, :1]).astype(BF16)


def _decode_attention(page_table, ql, qr, c_new, kr_new, r_new, cache_c, cache_krt, cache_rt, layer):
    nseq, n_pages = page_table.shape
    assert n_pages % DEC_PAGES == 0
    nch = n_pages // DEC_PAGES
    nsteps = nseq * nch
    keys = DEC_PAGES * PAGE_SIZE
    scratch = (2 * _nbytes((keys, KV_LORA), F32) + 2 * _nbytes((QK_ROPE, keys), F32)
               + 2 * _nbytes((N_HEADS, keys), F32) + 2 * _nbytes((N_HEADS, LANES), F32)
               + _nbytes((N_HEADS, KV_LORA), F32))
    blocks = 2 * _nbytes((N_HEADS, KV_LORA), BF16) + 3 * _nbytes((8, KV_LORA), F32)
    temps = 2 * _nbytes((keys, KV_LORA), BF16) + 6 * _nbytes((N_HEADS, keys), F32)
    per_seq = lambda i, pt: (i // nch, 0, 0)
    grid_spec = pltpu.PrefetchScalarGridSpec(
        num_scalar_prefetch=1,
        grid=(nsteps,),
        in_specs=[pl.BlockSpec((1, N_HEADS, KV_LORA), per_seq),
                  pl.BlockSpec((1, N_HEADS, QK_ROPE), per_seq),
                  pl.BlockSpec((1, 1, KV_LORA), per_seq),
                  pl.BlockSpec((1, 1, QK_ROPE), per_seq),
                  pl.BlockSpec((1, N_HEADS, 1), per_seq),
                  pl.BlockSpec(memory_space=pl.ANY),
                  pl.BlockSpec(memory_space=pl.ANY),
                  pl.BlockSpec(memory_space=pl.ANY)],
        out_specs=pl.BlockSpec((1, N_HEADS, KV_LORA), per_seq),
        scratch_shapes=[pltpu.VMEM((2, keys, KV_LORA), F32),
                        pltpu.VMEM((2, QK_ROPE, keys), F32),
                        pltpu.VMEM((2, N_HEADS, keys), F32),
                        pltpu.SemaphoreType.DMA((2, 3)),
                        pltpu.VMEM((N_HEADS, LANES), F32),
                        pltpu.VMEM((N_HEADS, LANES), F32),
                        pltpu.VMEM((N_HEADS, KV_LORA), F32)])
    return pl.pallas_call(
        functools.partial(_decode_kernel, layer=layer, n_pages=n_pages, nch=nch, nsteps=nsteps),
        grid_spec=grid_spec,
        out_shape=jax.ShapeDtypeStruct((nseq, N_HEADS, KV_LORA), BF16),
        compiler_params=pltpu.CompilerParams(
            dimension_semantics=("arbitrary",),
            vmem_limit_bytes=_vmem_limit(blocks, scratch, temps)),
        name="decode_attention",
    )(page_table.reshape(-1), ql, qr, c_new, kr_new, r_new, cache_c, cache_krt, cache_rt)


def _head_out_kernel(x_ref, w_ref, o_ref):
    o_ref[...] = jnp.dot(x_ref[...], w_ref[...], preferred_element_type=F32).astype(BF16)


def _head_out(o_lat, w_uv):
    m = o_lat.shape[0]
    return pl.pallas_call(
        _head_out_kernel,
        grid=(N_HEADS,),
        in_specs=[pl.BlockSpec((m, KV_LORA), lambda h: (0, h)),
                  pl.BlockSpec((KV_LORA, V_DIM), lambda h: (0, h))],
        out_specs=pl.BlockSpec((m, V_DIM), lambda h: (0, h)),
        out_shape=jax.ShapeDtypeStruct((m, N_HEADS * V_DIM), BF16),
        compiler_params=pltpu.CompilerParams(dimension_semantics=("parallel",)),
        name="decode_head_out",
    )(o_lat, w_uv)


def _merge_kernel(ya_ref, yb_ref, yc_ref, g0_ref, g1_ref, g2_ref, wa_ref, wb_ref, wc_ref, o_ref):
    a = jnp.dot(ya_ref[...], wa_ref[...], preferred_element_type=F32)
    b = jnp.dot(yb_ref[...], wb_ref[...], preferred_element_type=F32)
    c = jnp.dot(yc_ref[...], wc_ref[...], preferred_element_type=F32)
    merged = (jax.nn.sigmoid(g0_ref[...]) * a + jax.nn.sigmoid(g1_ref[...]) * b
              + jax.nn.sigmoid(g2_ref[...]) * c)
    o_ref[...] = merged.astype(BF16)


def _merge(ya, yb, yc, zg, w_a, w_b, w_c, layer, bm):
    m = ya.shape[0]
    bn = MERGE_BN
    gate = lambda j: pl.BlockSpec((bm, bn), lambda n, i: (i, j * (D_MODEL // bn) + n))
    blocks = (2 * _nbytes((bm, WIDTH_A), BF16) + _nbytes((bm, D_MODEL), BF16)
              + 3 * _nbytes((bm, bn), F32) + 2 * _nbytes((WIDTH_A, bn), BF16)
              + _nbytes((D_MODEL, bn), BF16) + _nbytes((bm, bn), BF16))
    temps = 5 * _nbytes((bm, bn), F32)
    return pl.pallas_call(
        _merge_kernel,
        grid=(D_MODEL // bn, m // bm),
        in_specs=[pl.BlockSpec((bm, WIDTH_A), lambda n, i: (i, 0)),
                  pl.BlockSpec((bm, WIDTH_B), lambda n, i: (i, 0)),
                  pl.BlockSpec((bm, N_HEADS * V_DIM), lambda n, i: (i, 0)),
                  gate(0), gate(1), gate(2),
                  pl.BlockSpec((None, WIDTH_A, bn), lambda n, i: (layer, 0, n)),
                  pl.BlockSpec((None, WIDTH_B, bn), lambda n, i: (layer, 0, n)),
                  pl.BlockSpec((None, N_HEADS * V_DIM, bn), lambda n, i: (layer, 0, n))],
        out_specs=pl.BlockSpec((bm, bn), lambda n, i: (i, n)),
        out_shape=jax.ShapeDtypeStruct((m, D_MODEL), BF16),
        compiler_params=pltpu.CompilerParams(
            dimension_semantics=("parallel", "parallel"),
            vmem_limit_bytes=_vmem_limit(blocks, 0, temps)),
        name="merge",
    )(ya, yb, yc, zg, zg, zg, w_a, w_b, w_c)


def _out_proj_kernel(x_ref, m_ref, w_ref, o_ref):
    o_ref[...] = x_ref[...] + jnp.dot(m_ref[...], w_ref[...], preferred_element_type=F32)


def _out_proj(x, merged, w, layer, bm):
    m = x.shape[0]
    blocks = (2 * _nbytes((bm, D_MODEL), F32) + _nbytes((bm, D_MODEL), BF16)
              + _nbytes((D_MODEL, D_MODEL), BF16))
    return pl.pallas_call(
        _out_proj_kernel,
        grid=(m // bm,),
        in_specs=[pl.BlockSpec((bm, D_MODEL), lambda i: (i, 0)),
                  pl.BlockSpec((bm, D_MODEL), lambda i: (i, 0)),
                  pl.BlockSpec((None, D_MODEL, D_MODEL), lambda i: (layer, 0, 0))],
        out_specs=pl.BlockSpec((bm, D_MODEL), lambda i: (i, 0)),
        out_shape=jax.ShapeDtypeStruct((m, D_MODEL), F32),
        compiler_params=pltpu.CompilerParams(
            dimension_semantics=("parallel",),
            vmem_limit_bytes=_vmem_limit(blocks, 0, _nbytes((bm, D_MODEL), F32))),
        name="out_proj",
    )(x, merged, w)


def _rope_tables(pos):
    half = QK_ROPE // 2
    inv = jnp.power(ROPE_THETA, -jnp.arange(half, dtype=F32) / half)
    ang = pos.astype(F32)[:, None] * inv[None, :]
    zeros = jnp.zeros((pos.shape[0], LANES - QK_ROPE), F32)
    cos_t = jnp.concatenate([jnp.cos(ang), jnp.cos(ang), zeros], axis=-1)
    sin_t = jnp.concatenate([jnp.sin(ang), jnp.sin(ang), zeros], axis=-1)
    return cos_t, sin_t


def kernel(x_prompt, x_sample, cache_latent, cache_krope, cache_kscale, state_pool, page_table, ffn1_norm, ffn1_up, ffn1_down, mix_norm, w_in, gmlp_v_norm, gmlp_ws, gmlp_b, pool_w, pool_scale, mla_q_norm, mla_w_uq, mla_kv_norm, mla_w_uk, mla_w_uv, mla_q_gain, mla_k_gain, w_o_a, w_o_b, w_o_c, w_out, ffn2_norm, ffn2_up, ffn2_down):
    batch, seq, _ = x_prompt.shape
    nseq = x_sample.shape[0]
    depth = w_in.shape[0]
    n_pages = page_table.shape[1]
    past = n_pages * cache_latent.shape[2]
    off_g = Z_KR + QK_ROPE

    cos_p, sin_p = _rope_tables(jnp.arange(seq))
    cos_s, sin_s = _rope_tables(jnp.full((nseq,), past))
    tril = jnp.tril(jnp.ones((CHUNK, CHUNK), dtype=bool))
    dga = WIDTH_A // GROUPS_A
    cache_krt = jnp.swapaxes(cache_krope, 2, 3)
    cache_rt = jnp.swapaxes(cache_kscale, 2, 3)

    up1, down1 = ffn1_up.astype(BF16), ffn1_down.astype(BF16)
    up2, down2 = ffn2_up.astype(BF16), ffn2_down.astype(BF16)
    w_mix = jnp.pad(w_in[:, :, :off_g], ((0, 0), (0, 0), (0, LANES - QK_ROPE))).astype(BF16)
    w_gates = w_in[:, :, off_g:].astype(BF16)
    woa, wob, woc, wout = w_o_a.astype(BF16), w_o_b.astype(BF16), w_o_c.astype(BF16), w_out.astype(BF16)

    yp = x_prompt.reshape(batch * seq, D_MODEL)
    ys = x_sample.reshape(nseq, D_MODEL)
    outs = {k: [] for k in ("lat_p", "kr_p", "r_p", "pool_p", "lat_s", "kr_s", "r_s", "pool_s", "v_s")}
    row = lambda a: a.reshape(1, -1)

    for l in range(depth):
        wq = jnp.pad(mla_w_uq[l], ((0, 0), (0, 0), (0, QPAD - QK_DIM))).reshape(Q_LORA, N_HEADS * QPAD).astype(BF16)
        lw = {
            "q_norm": row(mla_q_norm[l]), "kv_norm": row(mla_kv_norm[l]),
            "w_uq": wq,
            "w_uk": mla_w_uk[l].reshape(KV_LORA, N_HEADS * QK_NOPE).astype(BF16),
            "w_ukt": jnp.transpose(mla_w_uk[l], (1, 2, 0)).astype(BF16),
            "w_uv": mla_w_uv[l].reshape(KV_LORA, N_HEADS * V_DIM).astype(BF16),
            "q_gain": row(jnp.pad(mla_q_gain[l], (0, QPAD - QK_DIM))),
            "k_gain_nope": row(mla_k_gain[l][:QK_NOPE]),
            "k_gain_rope": row(jnp.pad(mla_k_gain[l][QK_NOPE:], (0, LANES - QK_ROPE))),
        }
        ws_tril = jnp.where(tril, gmlp_ws[l], 0).astype(BF16)
        pw = pool_w[l].astype(BF16)

        yp = _ffn(yp, row(ffn1_norm[l]), up1, down1, l, ROW_TILE)
        z = _in_proj(yp, row(mix_norm[l]), w_mix, l, IN_BM, IN_BN_MIX)
        zg = _in_proj(yp, row(mix_norm[l]), w_gates, l, IN_BM, IN_BN_GATES)
        ya = _gmlp_prompt(z, row(gmlp_v_norm[l]), ws_tril, gmlp_b[l].T, ROW_TILE)
        yb, pool_new = _pool_prompt(z, pw, row(pool_scale[l]), batch, seq, ROW_TILE)
        q, k, v, c, kr, r = _mla_prompt(z, lw, cos_p, sin_p, seq, MLA_BT)
        yc = _flash(q.reshape(batch, seq, -1), k.reshape(batch, seq, -1), v.reshape(batch, seq, -1), ATT_BLK)
        merged = _merge(ya, yb, yc.reshape(batch * seq, -1), zg, woa, wob, woc, l, ROW_TILE)
        yp = _out_proj(yp, merged, wout, l, ROW_TILE)
        yp = _ffn(yp, row(ffn2_norm[l]), up2, down2, l, ROW_TILE)
        outs["lat_p"].append(c.reshape(batch, seq, KV_LORA))
        outs["kr_p"].append(kr.reshape(batch, seq, QK_ROPE))
        outs["r_p"].append(r.reshape(batch, seq, N_HEADS))
        outs["pool_p"].append(pool_new)

        ys = _ffn(ys, row(ffn1_norm[l]), up1, down1, l, nseq)
        zs = _in_proj(ys, row(mix_norm[l]), w_mix, l, nseq, IN_BN_MIX)
        zgs = _in_proj(ys, row(mix_norm[l]), w_gates, l, nseq, IN_BN_GATES)
        w0 = jnp.repeat(gmlp_ws[l][:, 0, 0].astype(BF16).astype(F32), dga)
        b0 = jnp.repeat(gmlp_b[l][:, 0], dga)
        ya_s, v_s = _gmlp_sample(zs, row(gmlp_v_norm[l]), row(w0), row(b0))
        yb_s = _pool_sample(zs, jnp.swapaxes(state_pool[l], 0, 1), pw, row(pool_scale[l]), past)
        ql, qr, c_s, kr_s, r_s = _mla_sample(zs, lw, cos_s, sin_s)
        o_lat = _decode_attention(
            page_table, ql.reshape(nseq, N_HEADS, KV_LORA), qr.reshape(nseq, N_HEADS, QK_ROPE),
            c_s.reshape(nseq, 1, KV_LORA), kr_s.reshape(nseq, 1, QK_ROPE), r_s.reshape(nseq, N_HEADS, 1),
            cache_latent, cache_krt, cache_rt, l)
        yc_s = _head_out(o_lat.reshape(nseq, N_HEADS * KV_LORA), lw["w_uv"])
        merged_s = _merge(ya_s, yb_s, yc_s, zgs, woa, wob, woc, l, nseq)
        ys = _out_proj(ys, merged_s, wout, l, nseq)
        ys = _ffn(ys, row(ffn2_norm[l]), up2, down2, l, nseq)
        outs["lat_s"].append(c_s.reshape(nseq, 1, KV_LORA))
        outs["kr_s"].append(kr_s.reshape(nseq, 1, QK_ROPE))
        outs["r_s"].append(r_s.reshape(nseq, 1, N_HEADS))
        outs["pool_s"].append(jnp.concatenate(
            [state_pool[l][:, 1:], zs[:, Z_B:Z_B + WIDTH_B][:, None, :]], axis=1))
        outs["v_s"].append(v_s.reshape(nseq, 1, WIDTH_A))

    st = lambda key: jnp.stack(outs[key])
    return (yp.reshape(batch, seq, D_MODEL), ys.reshape(nseq, 1, D_MODEL),
            st("lat_p"), st("kr_p"), st("r_p"), st("pool_p"),
            st("lat_s"), st("kr_s"), st("r_s"), st("pool_s"), st("v_s"))
```
